```python
import jax
import jax.numpy as jnp
from jax import lax
import numpy as np


D_MODEL = 1024
BATCH = 32
SEQ = 2048
DEPTH = 4

GRID_W = 64
CTX_LEN = 256
HEAD_DIM = 128
N_HEADS = D_MODEL // HEAD_DIM
N_KV_HEADS = N_HEADS // 4
Q_GROUP = N_HEADS // N_KV_HEADS
Q_BLOCK = 128
ATTN_SCALE = HEAD_DIM ** -0.5
ROPE_THETA = 10000.0
ROPE_PAIRS = HEAD_DIM // 4
CONV_WIDTH = 31
CONV_CH = D_MODEL
POOL_WINDOWS = (2, 4, 8, 16)
POOL_GROUPS = len(POOL_WINDOWS)
POOL_CH = D_MODEL
POOL_GROUP_CH = POOL_CH // POOL_GROUPS
N_BRANCH = 3
D_FF = 2816
FFN_CONV_WIDTH = 3
N_MOD = 6
ATTN_W = N_HEADS * HEAD_DIM
KV_W = N_KV_HEADS * HEAD_DIM
Q_END = ATTN_W
K_END = Q_END + KV_W
V_END = K_END + KV_W
CONV_END = V_END + 2 * CONV_CH
POOL_END = CONV_END + POOL_CH
D_IN = POOL_END + N_BRANCH * D_MODEL
DEEPNORM_ALPHA = (2 * DEPTH) ** 0.25
DEEPNORM_BETA = (8 * DEPTH) ** -0.25
LN_EPS = 1e-5
RMS_EPS = 1e-6

kernel_name = "hybrid_gated_parallel_diffusion_trunk"


def layer_norm(t, g, b):
    tf = t.astype(jnp.float32)
    mu = jnp.mean(tf, axis=-1, keepdims=True)
    var = jnp.mean(jnp.square(tf - mu), axis=-1, keepdims=True)
    return ((tf - mu) * lax.rsqrt(var + LN_EPS)).astype(t.dtype) * g + b


def rms_norm(t, g):
    tf = t.astype(jnp.float32)
    ms = jnp.mean(jnp.square(tf), axis=-1, keepdims=True)
    return (tf * lax.rsqrt(ms + RMS_EPS)).astype(t.dtype) * g


def ada_mod(cond, w, b, n):
    m = jax.nn.silu(cond) @ w[:, :n * D_MODEL] + b[:n * D_MODEL]
    return m.reshape(m.shape[:-1] + (n, D_MODEL))


def modulate(h, shift, scale):
    return h * (1.0 + scale) + shift


def axial_rope_tables(n_tokens):
    n_rows = n_tokens // GRID_W
    row = jnp.repeat(jnp.arange(n_rows, dtype=jnp.float32), GRID_W)
    col = jnp.tile(jnp.arange(GRID_W, dtype=jnp.float32), n_rows)
    inv_freq = ROPE_THETA ** (-jnp.arange(ROPE_PAIRS, dtype=jnp.float32) / ROPE_PAIRS)
    ang = jnp.stack([row, col], axis=-1)[..., None] * inv_freq
    return jnp.cos(ang), jnp.sin(ang)


def apply_rope(t, cos, sin):
    b, n, h, _ = t.shape
    tf = t.astype(jnp.float32).reshape(b, n, h, 2, 2, ROPE_PAIRS)
    t1, t2 = tf[..., 0, :], tf[..., 1, :]
    c = cos[None, :, None]
    s = sin[None, :, None]
    out = jnp.stack([t1 * c - t2 * s, t1 * s + t2 * c], axis=-2)
    return out.reshape(t.shape).astype(t.dtype)


def kv_heads(z_kv, k_gain):
    lead = z_kv.shape[:-1]
    k = rms_norm(z_kv[..., :KV_W].reshape(lead + (N_KV_HEADS, HEAD_DIM)), k_gain)
    v = z_kv[..., KV_W:].reshape(lead + (N_KV_HEADS, HEAD_DIM))
    return k, v


def qkv_heads(z, q_gain, k_gain):
    q = rms_norm(z[..., :Q_END].reshape(z.shape[:-1] + (N_HEADS, HEAD_DIM)), q_gain)
    k, v = kv_heads(z[..., Q_END:V_END], k_gain)
    return q, k, v


def latent_attention(q, k_lat, v_lat, k_ctx, v_ctx):
    b, n = q.shape[0], q.shape[1]
    k_all = jnp.concatenate([k_ctx, k_lat], axis=1)
    v_all = jnp.concatenate([v_ctx, v_lat], axis=1)
    n_blk = n // Q_BLOCK
    qb = jnp.moveaxis(q.reshape(b, n_blk, Q_BLOCK, N_KV_HEADS, Q_GROUP, HEAD_DIM), 1, 0)

    def block(qi):
        s = jnp.einsum('bqhgd,bkhd->bhgqk', qi, k_all, preferred_element_type=jnp.float32) * ATTN_SCALE
        p = jax.nn.softmax(s, axis=-1).astype(v_all.dtype)
        return jnp.einsum('bhgqk,bkhd->bqhgd', p, v_all)

    o = lax.map(block, qb)
    return jnp.moveaxis(o, 0, 1).reshape(b, n, ATTN_W)


def context_attention(q, k, v):
    b, n = q.shape[0], q.shape[1]
    qg = q.reshape(b, n, N_KV_HEADS, Q_GROUP, HEAD_DIM)
    s = jnp.einsum('bqhgd,bkhd->bhgqk', qg, k, preferred_element_type=jnp.float32) * ATTN_SCALE
    p = jax.nn.softmax(s, axis=-1).astype(v.dtype)
    return jnp.einsum('bhgqk,bkhd->bqhgd', p, v).reshape(b, n, ATTN_W)


def depthwise_conv(t, w, b):
    k = w.shape[0]
    y = lax.conv_general_dilated(
        t, w[:, None, :], window_strides=(1,), padding=[((k - 1) // 2, k // 2)],
        dimension_numbers=('NWC', 'WIO', 'NWC'), feature_group_count=t.shape[-1])
    return y + b


def conformer_conv(u, dw_w, dw_b, ln_g, ln_b, pw_w, pw_b):
    a, gt = jnp.split(u, 2, axis=-1)
    h = depthwise_conv(a * jax.nn.sigmoid(gt), dw_w, dw_b)
    h = jax.nn.silu(layer_norm(h, ln_g, ln_b))
    return h @ pw_w + pw_b


def multiscale_pool(u, pool_w, pool_scale):
    b, n, _ = u.shape
    uf = u.astype(jnp.float32)
    cs = jnp.pad(jnp.cumsum(uf, axis=1), ((0, 0), (1, 0), (0, 0)))
    t = jnp.arange(n)
    outs = []
    for g, w in enumerate(POOL_WINDOWS):
        lo = jnp.clip(t - w // 2, 0, n)
        hi = jnp.clip(t - w // 2 + w, 0, n)
        sl = slice(g * POOL_GROUP_CH, (g + 1) * POOL_GROUP_CH)
        seg = cs[:, :, sl]
        win_sum = jnp.take(seg, hi, axis=1) - jnp.take(seg, lo, axis=1)
        cnt = (hi - lo).astype(jnp.float32)[None, :, None]
        outs.append(win_sum / cnt - uf[:, :, sl])
    pooled = jnp.stack(outs, axis=2).astype(u.dtype)
    mixed = jnp.einsum('blgi,gio->blgo', pooled, pool_w)
    return mixed.reshape(b, n, POOL_CH) * pool_scale


def merge_branches(z, attn_o, conv_dw_w, conv_dw_b, conv_ln_g, conv_ln_b, conv_pw_w, conv_pw_b,
                   pool_w, pool_scale, w_out, b_out):
    conv_o = conformer_conv(z[..., V_END:CONV_END], conv_dw_w, conv_dw_b, conv_ln_g, conv_ln_b,
                            conv_pw_w, conv_pw_b)
    pool_o = multiscale_pool(z[..., CONV_END:POOL_END], pool_w, pool_scale)
    gates = jax.nn.sigmoid(z[..., POOL_END:].reshape(z.shape[:-1] + (N_BRANCH, D_MODEL)))
    m = gates[..., 0, :] * attn_o + gates[..., 1, :] * conv_o + gates[..., 2, :] * pool_o
    return m @ w_out + b_out


def conv_ffn(h, w_up, dw_w, dw_b, w_down):
    a, u = jnp.split(h @ w_up, 2, axis=-1)
    a = depthwise_conv(a, dw_w, dw_b)
    return (jax.nn.silu(a) * u) @ w_down


def _fwd_setup_inputs(seed: int = 0) -> dict:
    key = jax.random.key(seed)
    ks = iter(jax.random.split(key, 32))

    def nrm(shape, scale):
        return jax.random.normal(next(ks), shape, jnp.float32) * scale

    L = DEPTH
    D = D_MODEL
    return {
        'x': nrm((BATCH, SEQ, D), 1.0),
        'c': nrm((BATCH, D), 1.0),
        'ctx': nrm((BATCH, CTX_LEN, D), 1.0),
        'c_ctx': nrm((D,), 1.0),
        'w_ada': nrm((L, D, N_MOD * D), 0.5 * D ** -0.5),
        'b_ada': nrm((L, N_MOD * D), 0.02),
        'w_in': nrm((L, D, D_IN), D ** -0.5),
        'b_in': nrm((L, D_IN), 0.02),
        'q_gain': 1.0 + nrm((L, HEAD_DIM), 0.02),
        'k_gain': 1.0 + nrm((L, HEAD_DIM), 0.02),
        'conv_dw_w': nrm((L, CONV_WIDTH, CONV_CH), CONV_WIDTH ** -0.5),
        'conv_dw_b': nrm((L, CONV_CH), 0.02),
        'conv_ln_g': 1.0 + nrm((L, CONV_CH), 0.02),
        'conv_ln_b': nrm((L, CONV_CH), 0.02),
        'conv_pw_w': nrm((L, CONV_CH, D), CONV_CH ** -0.5),
        'conv_pw_b': nrm((L, D), 0.02),
        'pool_w': nrm((L, POOL_GROUPS, POOL_GROUP_CH, POOL_GROUP_CH), POOL_GROUP_CH ** -0.5),
        'pool_scale': 1.0 + nrm((L, POOL_CH), 0.1),
        'w_out': nrm((L, D, D), DEEPNORM_BETA * D ** -0.5),
        'b_out': nrm((L, D), 0.02),
        'ln1_g': 1.0 + nrm((L, D), 0.02),
        'ln1_b': nrm((L, D), 0.02),
        'ln2_g': 1.0 + nrm((L, D), 0.02),
        'ln2_b': nrm((L, D), 0.02),
        'w_up': nrm((L, D, 2 * D_FF), D ** -0.5),
        'ffn_dw_w': nrm((L, FFN_CONV_WIDTH, D_FF), FFN_CONV_WIDTH ** -0.5),
        'ffn_dw_b': nrm((L, D_FF), 0.02),
        'w_down': nrm((L, D_FF, D), DEEPNORM_BETA * D_FF ** -0.5),
    }


def _fwd_reference(x, c, ctx, c_ctx, w_ada, b_ada, w_in, b_in, q_gain, k_gain,
              conv_dw_w, conv_dw_b, conv_ln_g, conv_ln_b, conv_pw_w, conv_pw_b,
              pool_w, pool_scale, w_out, b_out, ln1_g, ln1_b, ln2_g, ln2_b,
              w_up, ffn_dw_w, ffn_dw_b, w_down):
    cos, sin = axial_rope_tables(x.shape[1])
    for l in range(DEPTH):
        last = l == DEPTH - 1

        def mix(z, attn_o):
            return merge_branches(z, attn_o, conv_dw_w[l], conv_dw_b[l], conv_ln_g[l], conv_ln_b[l],
                                  conv_pw_w[l], conv_pw_b[l], pool_w[l], pool_scale[l],
                                  w_out[l], b_out[l])

        def ffn(h):
            return conv_ffn(h, w_up[l], ffn_dw_w[l], ffn_dw_b[l], w_down[l])

        ml = ada_mod(c, w_ada[l], b_ada[l], N_MOD)[:, None]
        mc = ada_mod(c_ctx, w_ada[l], b_ada[l], 2 if last else N_MOD)

        hl = modulate(x, ml[..., 0, :], ml[..., 1, :])
        hc = modulate(ctx, mc[0], mc[1])
        zl = hl @ w_in[l] + b_in[l]
        q_l, k_l, v_l = qkv_heads(zl, q_gain[l], k_gain[l])
        q_l = apply_rope(q_l, cos, sin)
        k_l = apply_rope(k_l, cos, sin)
        if last:
            k_c, v_c = kv_heads(hc @ w_in[l][:, Q_END:V_END] + b_in[l][Q_END:V_END], k_gain[l])
        else:
            zc = hc @ w_in[l] + b_in[l]
            q_c, k_c, v_c = qkv_heads(zc, q_gain[l], k_gain[l])
        attn_l = latent_attention(q_l, k_l, v_l, k_c, v_c)
        x = layer_norm(DEEPNORM_ALPHA * x + ml[..., 2, :] * mix(zl, attn_l), ln1_g[l], ln1_b[l])

        x = layer_norm(DEEPNORM_ALPHA * x + ml[..., 5, :] * ffn(modulate(x, ml[..., 3, :], ml[..., 4, :])),
                       ln2_g[l], ln2_b[l])

        if not last:
            attn_c = context_attention(q_c, k_c, v_c)
            ctx = layer_norm(DEEPNORM_ALPHA * ctx + mc[2] * mix(zc, attn_c), ln1_g[l], ln1_b[l])
            ctx = layer_norm(DEEPNORM_ALPHA * ctx + mc[5] * ffn(modulate(ctx, mc[3], mc[4])),
                             ln2_g[l], ln2_b[l])
    return x


import jax as _jax
import jax.numpy as _jnp

TWIN_FORMAT = 'train_step'
FWD_PARAMS = ['x', 'c', 'ctx', 'c_ctx', 'w_ada', 'b_ada', 'w_in', 'b_in', 'q_gain', 'k_gain', 'conv_dw_w', 'conv_dw_b', 'conv_ln_g', 'conv_ln_b', 'conv_pw_w', 'conv_pw_b', 'pool_w', 'pool_scale', 'w_out', 'b_out', 'ln1_g', 'ln1_b', 'ln2_g', 'ln2_b', 'w_up', 'ffn_dw_w', 'ffn_dw_b', 'w_down']
TWIN_WEIGHTS = ['c_ctx', 'w_ada', 'b_ada', 'w_in', 'b_in', 'q_gain', 'k_gain', 'conv_dw_w', 'conv_dw_b', 'conv_ln_g', 'conv_ln_b', 'conv_pw_w', 'conv_pw_b', 'pool_w', 'pool_scale', 'w_out', 'b_out', 'ln1_g', 'ln1_b', 'ln2_g', 'ln2_b', 'w_up', 'ffn_dw_w', 'ffn_dw_b', 'w_down']
TWIN_DIFF_INPUT = 'x'
TWIN_INPUTS = ['x', 'c', 'ctx', 'c_ctx', 'w_ada', 'b_ada', 'w_in', 'b_in', 'q_gain', 'k_gain', 'conv_dw_w', 'conv_dw_b', 'conv_ln_g', 'conv_ln_b', 'conv_pw_w', 'conv_pw_b', 'pool_w', 'pool_scale', 'w_out', 'b_out', 'ln1_g', 'ln1_b', 'ln2_g', 'ln2_b', 'w_up', 'ffn_dw_w', 'ffn_dw_b', 'w_down', 'loss_target', 'm_c_ctx', 'm_w_ada', 'm_b_ada', 'm_w_in', 'm_b_in', 'm_q_gain', 'm_k_gain', 'm_conv_dw_w', 'm_conv_dw_b', 'm_conv_ln_g', 'm_conv_ln_b', 'm_conv_pw_w', 'm_conv_pw_b', 'm_pool_w', 'm_pool_scale', 'm_w_out', 'm_b_out', 'm_ln1_g', 'm_ln1_b', 'm_ln2_g', 'm_ln2_b', 'm_w_up', 'm_ffn_dw_w', 'm_ffn_dw_b', 'm_w_down', 'v_c_ctx', 'v_w_ada', 'v_b_ada', 'v_w_in', 'v_b_in', 'v_q_gain', 'v_k_gain', 'v_conv_dw_w', 'v_conv_dw_b', 'v_conv_ln_g', 'v_conv_ln_b', 'v_conv_pw_w', 'v_conv_pw_b', 'v_pool_w', 'v_pool_scale', 'v_w_out', 'v_b_out', 'v_ln1_g', 'v_ln1_b', 'v_ln2_g', 'v_ln2_b', 'v_w_up', 'v_ffn_dw_w', 'v_ffn_dw_b', 'v_w_down']
TWIN_OUTPUTS = ['loss', 'grad_x', 'grad_c_ctx', 'grad_w_ada', 'grad_b_ada', 'grad_w_in', 'grad_b_in', 'grad_q_gain', 'grad_k_gain', 'grad_conv_dw_w', 'grad_conv_dw_b', 'grad_conv_ln_g', 'grad_conv_ln_b', 'grad_conv_pw_w', 'grad_conv_pw_b', 'grad_pool_w', 'grad_pool_scale', 'grad_w_out', 'grad_b_out', 'grad_ln1_g', 'grad_ln1_b', 'grad_ln2_g', 'grad_ln2_b', 'grad_w_up', 'grad_ffn_dw_w', 'grad_ffn_dw_b', 'grad_w_down', 'delta_c_ctx', 'delta_w_ada', 'delta_b_ada', 'delta_w_in', 'delta_b_in', 'delta_q_gain', 'delta_k_gain', 'delta_conv_dw_w', 'delta_conv_dw_b', 'delta_conv_ln_g', 'delta_conv_ln_b', 'delta_conv_pw_w', 'delta_conv_pw_b', 'delta_pool_w', 'delta_pool_scale', 'delta_w_out', 'delta_b_out', 'delta_ln1_g', 'delta_ln1_b', 'delta_ln2_g', 'delta_ln2_b', 'delta_w_up', 'delta_ffn_dw_w', 'delta_ffn_dw_b', 'delta_w_down', 'new_m_c_ctx', 'new_m_w_ada', 'new_m_b_ada', 'new_m_w_in', 'new_m_b_in', 'new_m_q_gain', 'new_m_k_gain', 'new_m_conv_dw_w', 'new_m_conv_dw_b', 'new_m_conv_ln_g', 'new_m_conv_ln_b', 'new_m_conv_pw_w', 'new_m_conv_pw_b', 'new_m_pool_w', 'new_m_pool_scale', 'new_m_w_out', 'new_m_b_out', 'new_m_ln1_g', 'new_m_ln1_b', 'new_m_ln2_g', 'new_m_ln2_b', 'new_m_w_up', 'new_m_ffn_dw_w', 'new_m_ffn_dw_b', 'new_m_w_down', 'new_v_c_ctx', 'new_v_w_ada', 'new_v_b_ada', 'new_v_w_in', 'new_v_b_in', 'new_v_q_gain', 'new_v_k_gain', 'new_v_conv_dw_w', 'new_v_conv_dw_b', 'new_v_conv_ln_g', 'new_v_conv_ln_b', 'new_v_conv_pw_w', 'new_v_conv_pw_b', 'new_v_pool_w', 'new_v_pool_scale', 'new_v_w_out', 'new_v_b_out', 'new_v_ln1_g', 'new_v_ln1_b', 'new_v_ln2_g', 'new_v_ln2_b', 'new_v_w_up', 'new_v_ffn_dw_w', 'new_v_ffn_dw_b', 'new_v_w_down']
TWIN_LEAF_KINDS = {'loss': 'loss', 'grad_x': 'grad_x', 'grad_c_ctx': 'grad_w', 'grad_w_ada': 'grad_w', 'grad_b_ada': 'grad_w', 'grad_w_in': 'grad_w', 'grad_b_in': 'grad_w', 'grad_q_gain': 'grad_w', 'grad_k_gain': 'grad_w', 'grad_conv_dw_w': 'grad_w', 'grad_conv_dw_b': 'grad_w', 'grad_conv_ln_g': 'grad_w', 'grad_conv_ln_b': 'grad_w', 'grad_conv_pw_w': 'grad_w', 'grad_conv_pw_b': 'grad_w', 'grad_pool_w': 'grad_w', 'grad_pool_scale': 'grad_w', 'grad_w_out': 'grad_w', 'grad_b_out': 'grad_w', 'grad_ln1_g': 'grad_w', 'grad_ln1_b': 'grad_w', 'grad_ln2_g': 'grad_w', 'grad_ln2_b': 'grad_w', 'grad_w_up': 'grad_w', 'grad_ffn_dw_w': 'grad_w', 'grad_ffn_dw_b': 'grad_w', 'grad_w_down': 'grad_w', 'delta_c_ctx': 'delta_w', 'delta_w_ada': 'delta_w', 'delta_b_ada': 'delta_w', 'delta_w_in': 'delta_w', 'delta_b_in': 'delta_w', 'delta_q_gain': 'delta_w', 'delta_k_gain': 'delta_w', 'delta_conv_dw_w': 'delta_w', 'delta_conv_dw_b': 'delta_w', 'delta_conv_ln_g': 'delta_w', 'delta_conv_ln_b': 'delta_w', 'delta_conv_pw_w': 'delta_w', 'delta_conv_pw_b': 'delta_w', 'delta_pool_w': 'delta_w', 'delta_pool_scale': 'delta_w', 'delta_w_out': 'delta_w', 'delta_b_out': 'delta_w', 'delta_ln1_g': 'delta_w', 'delta_ln1_b': 'delta_w', 'delta_ln2_g': 'delta_w', 'delta_ln2_b': 'delta_w', 'delta_w_up': 'delta_w', 'delta_ffn_dw_w': 'delta_w', 'delta_ffn_dw_b': 'delta_w', 'delta_w_down': 'delta_w', 'new_m_c_ctx': 'new_m', 'new_m_w_ada': 'new_m', 'new_m_b_ada': 'new_m', 'new_m_w_in': 'new_m', 'new_m_b_in': 'new_m', 'new_m_q_gain': 'new_m', 'new_m_k_gain': 'new_m', 'new_m_conv_dw_w': 'new_m', 'new_m_conv_dw_b': 'new_m', 'new_m_conv_ln_g': 'new_m', 'new_m_conv_ln_b': 'new_m', 'new_m_conv_pw_w': 'new_m', 'new_m_conv_pw_b': 'new_m', 'new_m_pool_w': 'new_m', 'new_m_pool_scale': 'new_m', 'new_m_w_out': 'new_m', 'new_m_b_out': 'new_m', 'new_m_ln1_g': 'new_m', 'new_m_ln1_b': 'new_m', 'new_m_ln2_g': 'new_m', 'new_m_ln2_b': 'new_m', 'new_m_w_up': 'new_m', 'new_m_ffn_dw_w': 'new_m', 'new_m_ffn_dw_b': 'new_m', 'new_m_w_down': 'new_m', 'new_v_c_ctx': 'new_v', 'new_v_w_ada': 'new_v', 'new_v_b_ada': 'new_v', 'new_v_w_in': 'new_v', 'new_v_b_in': 'new_v', 'new_v_q_gain': 'new_v', 'new_v_k_gain': 'new_v', 'new_v_conv_dw_w': 'new_v', 'new_v_conv_dw_b': 'new_v', 'new_v_conv_ln_g': 'new_v', 'new_v_conv_ln_b': 'new_v', 'new_v_conv_pw_w': 'new_v', 'new_v_conv_pw_b': 'new_v', 'new_v_pool_w': 'new_v', 'new_v_pool_scale': 'new_v', 'new_v_w_out': 'new_v', 'new_v_b_out': 'new_v', 'new_v_ln1_g': 'new_v', 'new_v_ln1_b': 'new_v', 'new_v_ln2_g': 'new_v', 'new_v_ln2_b': 'new_v', 'new_v_w_up': 'new_v', 'new_v_ffn_dw_w': 'new_v', 'new_v_ffn_dw_b': 'new_v', 'new_v_w_down': 'new_v'}


def _forward(args):
    return _fwd_reference(*[args[k] for k in FWD_PARAMS])


def _output_shape():
    out = _jax.eval_shape(lambda: _forward(_fwd_setup_inputs(0)))
    return out.shape, out.dtype

N_MICROBATCH = 1
ADAM_LR = 0.001
ADAM_B1 = 0.9
ADAM_B2 = 0.999
ADAM_EPS = 1e-08
ADAM_WD = 0.01
ADAM_STEP = 10
PER_EXAMPLE_BATCH_AXIS = {'x': 0, 'c': 0, 'ctx': 0, 'loss_target': 0}
SHARED_INPUTS = []
_WEIGHT_DTYPES = {'c_ctx': _jnp.float32, 'w_ada': _jnp.float32, 'b_ada': _jnp.float32, 'w_in': _jnp.float32, 'b_in': _jnp.float32, 'q_gain': _jnp.float32, 'k_gain': _jnp.float32, 'conv_dw_w': _jnp.float32, 'conv_dw_b': _jnp.float32, 'conv_ln_g': _jnp.float32, 'conv_ln_b': _jnp.float32, 'conv_pw_w': _jnp.float32, 'conv_pw_b': _jnp.float32, 'pool_w': _jnp.float32, 'pool_scale': _jnp.float32, 'w_out': _jnp.float32, 'b_out': _jnp.float32, 'ln1_g': _jnp.float32, 'ln1_b': _jnp.float32, 'ln2_g': _jnp.float32, 'ln2_b': _jnp.float32, 'w_up': _jnp.float32, 'ffn_dw_w': _jnp.float32, 'ffn_dw_b': _jnp.float32, 'w_down': _jnp.float32}
MOMENT_SCALE = {'c_ctx': 3.720202e-03, 'w_ada': 1.830992e-02, 'b_ada': 3.003490e-02, 'w_in': 5.034052e-03, 'b_in': 6.804310e-03, 'q_gain': 2.633546e-03, 'k_gain': 2.676997e-03, 'conv_dw_w': 6.670323e-03, 'conv_dw_b': 1.408109e-02, 'conv_ln_g': 8.661581e-03, 'conv_ln_b': 8.612641e-03, 'conv_pw_w': 6.882030e-03, 'conv_pw_b': 1.594394e-02, 'pool_w': 9.736360e-03, 'pool_scale': 9.710608e-03, 'w_out': 3.006534e-02, 'b_out': 7.749395e-02, 'ln1_g': 1.743347e+00, 'ln1_b': 7.461439e-01, 'ln2_g': 3.214594e+01, 'ln2_b': 1.502043e+00, 'w_up': 8.902429e-03, 'ffn_dw_w': 8.968796e-03, 'ffn_dw_b': 8.150935e-03, 'w_down': 3.468331e-02}


def _to_microbatches(a, axis):
    t = _jnp.moveaxis(a, axis, 0)
    t = t.reshape((N_MICROBATCH, t.shape[0] // N_MICROBATCH) + t.shape[1:])
    return _jnp.moveaxis(t, 1, axis + 1)


def setup_inputs(seed: int = 0) -> dict:
    inp = _fwd_setup_inputs(seed)
    key = _jax.random.fold_in(_jax.random.key(seed), 7919)
    shape, _ = _output_shape()
    out = dict(inp)
    out["loss_target"] = _jax.random.normal(_jax.random.fold_in(key, 0), shape, _jnp.float32)
    for i, name in enumerate(TWIN_WEIGHTS):
        w = inp[name].astype(_jnp.float32)
        if MOMENT_SCALE is None:
            s = _jnp.sqrt(_jnp.mean(_jnp.square(w)) + 1e-30)
        else:
            s = MOMENT_SCALE[name]
        km, kv = _jax.random.split(_jax.random.fold_in(key, i + 1))
        out[name] = w
        out["m_" + name] = s * _jax.random.normal(km, w.shape, _jnp.float32)
        out["v_" + name] = (s * s) * _jax.random.uniform(kv, w.shape, _jnp.float32, 0.5, 1.5)
    if N_MICROBATCH > 1:
        for name, axis in PER_EXAMPLE_BATCH_AXIS.items():
            out[name] = _to_microbatches(out[name], axis)
    return {'x': out['x'], 'c': out['c'], 'ctx': out['ctx'], 'c_ctx': out['c_ctx'], 'w_ada': out['w_ada'], 'b_ada': out['b_ada'], 'w_in': out['w_in'], 'b_in': out['b_in'], 'q_gain': out['q_gain'], 'k_gain': out['k_gain'], 'conv_dw_w': out['conv_dw_w'], 'conv_dw_b': out['conv_dw_b'], 'conv_ln_g': out['conv_ln_g'], 'conv_ln_b': out['conv_ln_b'], 'conv_pw_w': out['conv_pw_w'], 'conv_pw_b': out['conv_pw_b'], 'pool_w': out['pool_w'], 'pool_scale': out['pool_scale'], 'w_out': out['w_out'], 'b_out': out['b_out'], 'ln1_g': out['ln1_g'], 'ln1_b': out['ln1_b'], 'ln2_g': out['ln2_g'], 'ln2_b': out['ln2_b'], 'w_up': out['w_up'], 'ffn_dw_w': out['ffn_dw_w'], 'ffn_dw_b': out['ffn_dw_b'], 'w_down': out['w_down'], 'loss_target': out['loss_target'], 'm_c_ctx': out['m_c_ctx'], 'm_w_ada': out['m_w_ada'], 'm_b_ada': out['m_b_ada'], 'm_w_in': out['m_w_in'], 'm_b_in': out['m_b_in'], 'm_q_gain': out['m_q_gain'], 'm_k_gain': out['m_k_gain'], 'm_conv_dw_w': out['m_conv_dw_w'], 'm_conv_dw_b': out['m_conv_dw_b'], 'm_conv_ln_g': out['m_conv_ln_g'], 'm_conv_ln_b': out['m_conv_ln_b'], 'm_conv_pw_w': out['m_conv_pw_w'], 'm_conv_pw_b': out['m_conv_pw_b'], 'm_pool_w': out['m_pool_w'], 'm_pool_scale': out['m_pool_scale'], 'm_w_out': out['m_w_out'], 'm_b_out': out['m_b_out'], 'm_ln1_g': out['m_ln1_g'], 'm_ln1_b': out['m_ln1_b'], 'm_ln2_g': out['m_ln2_g'], 'm_ln2_b': out['m_ln2_b'], 'm_w_up': out['m_w_up'], 'm_ffn_dw_w': out['m_ffn_dw_w'], 'm_ffn_dw_b': out['m_ffn_dw_b'], 'm_w_down': out['m_w_down'], 'v_c_ctx': out['v_c_ctx'], 'v_w_ada': out['v_w_ada'], 'v_b_ada': out['v_b_ada'], 'v_w_in': out['v_w_in'], 'v_b_in': out['v_b_in'], 'v_q_gain': out['v_q_gain'], 'v_k_gain': out['v_k_gain'], 'v_conv_dw_w': out['v_conv_dw_w'], 'v_conv_dw_b': out['v_conv_dw_b'], 'v_conv_ln_g': out['v_conv_ln_g'], 'v_conv_ln_b': out['v_conv_ln_b'], 'v_conv_pw_w': out['v_conv_pw_w'], 'v_conv_pw_b': out['v_conv_pw_b'], 'v_pool_w': out['v_pool_w'], 'v_pool_scale': out['v_pool_scale'], 'v_w_out': out['v_w_out'], 'v_b_out': out['v_b_out'], 'v_ln1_g': out['v_ln1_g'], 'v_ln1_b': out['v_ln1_b'], 'v_ln2_g': out['v_ln2_g'], 'v_ln2_b': out['v_ln2_b'], 'v_w_up': out['v_w_up'], 'v_ffn_dw_w': out['v_ffn_dw_w'], 'v_ffn_dw_b': out['v_ffn_dw_b'], 'v_w_down': out['v_w_down']}


def _loss(weights, diff, rest, loss_target):
    with _jax.named_scope("forward"):
        args = {**rest, TWIN_DIFF_INPUT: diff, **{k: w.astype(_WEIGHT_DTYPES[k]) for k, w in weights.items()}}
        y = _forward(args)
    with _jax.named_scope("loss_head"):
        err = _jnp.square(y.astype(_jnp.float32) - loss_target)
        return 0.5 * _jnp.sum(_jnp.mean(err, axis=-1)) if err.ndim else 0.5 * err


def _adamw(w, g, m, v):
    m = ADAM_B1 * m + (1.0 - ADAM_B1) * g
    v = ADAM_B2 * v + (1.0 - ADAM_B2) * _jnp.square(g)
    m_hat = m / (1.0 - ADAM_B1 ** ADAM_STEP)
    v_hat = v / (1.0 - ADAM_B2 ** ADAM_STEP)
    delta = -ADAM_LR * (m_hat / (_jnp.sqrt(v_hat) + ADAM_EPS) + ADAM_WD * w)
    return delta, m, v


def reference(x, c, ctx, c_ctx, w_ada, b_ada, w_in, b_in, q_gain, k_gain, conv_dw_w, conv_dw_b, conv_ln_g, conv_ln_b, conv_pw_w, conv_pw_b, pool_w, pool_scale, w_out, b_out, ln1_g, ln1_b, ln2_g, ln2_b, w_up, ffn_dw_w, ffn_dw_b, w_down, loss_target, m_c_ctx, m_w_ada, m_b_ada, m_w_in, m_b_in, m_q_gain, m_k_gain, m_conv_dw_w, m_conv_dw_b, m_conv_ln_g, m_conv_ln_b, m_conv_pw_w, m_conv_pw_b, m_pool_w, m_pool_scale, m_w_out, m_b_out, m_ln1_g, m_ln1_b, m_ln2_g, m_ln2_b, m_w_up, m_ffn_dw_w, m_ffn_dw_b, m_w_down, v_c_ctx, v_w_ada, v_b_ada, v_w_in, v_b_in, v_q_gain, v_k_gain, v_conv_dw_w, v_conv_dw_b, v_conv_ln_g, v_conv_ln_b, v_conv_pw_w, v_conv_pw_b, v_pool_w, v_pool_scale, v_w_out, v_b_out, v_ln1_g, v_ln1_b, v_ln2_g, v_ln2_b, v_w_up, v_ffn_dw_w, v_ffn_dw_b, v_w_down):
    given = dict(x=x, c=c, ctx=ctx, c_ctx=c_ctx, w_ada=w_ada, b_ada=b_ada, w_in=w_in, b_in=b_in, q_gain=q_gain, k_gain=k_gain, conv_dw_w=conv_dw_w, conv_dw_b=conv_dw_b, conv_ln_g=conv_ln_g, conv_ln_b=conv_ln_b, conv_pw_w=conv_pw_w, conv_pw_b=conv_pw_b, pool_w=pool_w, pool_scale=pool_scale, w_out=w_out, b_out=b_out, ln1_g=ln1_g, ln1_b=ln1_b, ln2_g=ln2_g, ln2_b=ln2_b, w_up=w_up, ffn_dw_w=ffn_dw_w, ffn_dw_b=ffn_dw_b, w_down=w_down, loss_target=loss_target, m_c_ctx=m_c_ctx, m_w_ada=m_w_ada, m_b_ada=m_b_ada, m_w_in=m_w_in, m_b_in=m_b_in, m_q_gain=m_q_gain, m_k_gain=m_k_gain, m_conv_dw_w=m_conv_dw_w, m_conv_dw_b=m_conv_dw_b, m_conv_ln_g=m_conv_ln_g, m_conv_ln_b=m_conv_ln_b, m_conv_pw_w=m_conv_pw_w, m_conv_pw_b=m_conv_pw_b, m_pool_w=m_pool_w, m_pool_scale=m_pool_scale, m_w_out=m_w_out, m_b_out=m_b_out, m_ln1_g=m_ln1_g, m_ln1_b=m_ln1_b, m_ln2_g=m_ln2_g, m_ln2_b=m_ln2_b, m_w_up=m_w_up, m_ffn_dw_w=m_ffn_dw_w, m_ffn_dw_b=m_ffn_dw_b, m_w_down=m_w_down, v_c_ctx=v_c_ctx, v_w_ada=v_w_ada, v_b_ada=v_b_ada, v_w_in=v_w_in, v_b_in=v_b_in, v_q_gain=v_q_gain, v_k_gain=v_k_gain, v_conv_dw_w=v_conv_dw_w, v_conv_dw_b=v_conv_dw_b, v_conv_ln_g=v_conv_ln_g, v_conv_ln_b=v_conv_ln_b, v_conv_pw_w=v_conv_pw_w, v_conv_pw_b=v_conv_pw_b, v_pool_w=v_pool_w, v_pool_scale=v_pool_scale, v_w_out=v_w_out, v_b_out=v_b_out, v_ln1_g=v_ln1_g, v_ln1_b=v_ln1_b, v_ln2_g=v_ln2_g, v_ln2_b=v_ln2_b, v_w_up=v_w_up, v_ffn_dw_w=v_ffn_dw_w, v_ffn_dw_b=v_ffn_dw_b, v_w_down=v_w_down)
    weights = {n: given[n] for n in TWIN_WEIGHTS}
    shared = {n: given[n] for n in SHARED_INPUTS}
    per_example = {n: given[n] for n in ['x', 'c', 'ctx']}
    grad_fn = _jax.value_and_grad(_loss, argnums=(0, 1))

    def one_microbatch(ex, loss_target):
        ex = dict(ex)
        diff = ex.pop(TWIN_DIFF_INPUT)
        return grad_fn(weights, diff, {**shared, **ex}, loss_target)

    if N_MICROBATCH == 1:
        loss, (grad_w, grad_x) = one_microbatch(per_example, given["loss_target"])
    else:
        def body(carry, xs):
            loss_sum, grad_sum = carry
            l_k, (gw_k, gx_k) = one_microbatch(xs[0], xs[1])
            with _jax.named_scope("update"):
                return (loss_sum + l_k, _jax.tree.map(_jnp.add, grad_sum, gw_k)), gx_k

        init = (_jnp.zeros((), _jnp.float32), _jax.tree.map(_jnp.zeros_like, weights))
        (loss, grad_w), grad_x = _jax.lax.scan(body, init, (per_example, given["loss_target"]))
    with _jax.named_scope("update"):
        delta_w, new_m, new_v = {}, {}, {}
        for n in TWIN_WEIGHTS:
            delta_w[n], new_m[n], new_v[n] = _adamw(weights[n], grad_w[n], given["m_" + n], given["v_" + n])
    return (loss, grad_x, *[grad_w[n] for n in TWIN_WEIGHTS], *[delta_w[n] for n in TWIN_WEIGHTS],
            *[new_m[n] for n in TWIN_WEIGHTS], *[new_v[n] for n in TWIN_WEIGHTS])
```

```python
import functools
import math

import jax
import jax.numpy as jnp
from jax import lax
from jax.experimental import pallas as pl
from jax.experimental.pallas import tpu as pltpu

F32, BF16 = jnp.float32, jnp.bfloat16
MESH = pl.DeviceIdType.MESH
VMEM_LIMIT_BYTES = 48 * 1024 * 1024
GRID_W = 64
HEAD_DIM = 128
ROPE_THETA = 10000.0
ROPE_PAIRS = HEAD_DIM // 4
POOL_WINDOWS = (2, 4, 8, 16)
LN_EPS = 1e-5
RMS_EPS = 1e-6
ADAM_LR, ADAM_B1, ADAM_B2, ADAM_EPS, ADAM_WD, ADAM_STEP = 0.001, 0.9, 0.999, 1e-08, 0.01, 10
N_CHIPS, N_DEV = 4, 8

W_NAMES = ['c_ctx', 'w_ada', 'b_ada', 'w_in', 'b_in', 'q_gain', 'k_gain', 'conv_dw_w', 'conv_dw_b', 'conv_ln_g',
           'conv_ln_b', 'conv_pw_w', 'conv_pw_b', 'pool_w', 'pool_scale', 'w_out', 'b_out', 'ln1_g', 'ln1_b', 'ln2_g',
           'ln2_b', 'w_up', 'ffn_dw_w', 'ffn_dw_b', 'w_down']
BIG = {'w_in': 2, 'conv_pw_w': 1, 'pool_w': 2, 'w_out': 1, 'w_up': 2, 'w_down': 1}
SMALL_SHARDED = {'conv_dw_w': 2, 'ffn_dw_w': 2}
SMALL = ['b_in', 'q_gain', 'k_gain', 'conv_dw_w', 'conv_dw_b', 'conv_ln_g', 'conv_ln_b', 'conv_pw_b', 'pool_scale',
         'b_out', 'ln1_g', 'ln1_b', 'ln2_g', 'ln2_b', 'ffn_dw_w', 'ffn_dw_b']


def _tile(n, pref, align):
    if n <= pref:
        return n
    t = (pref // align) * align
    while t >= align:
        if n % t == 0:
            return t
        t -= align
    return n


def _call(body, *, name, out_shape, in_specs, out_specs, grid=None, scratch=()):
    kw = {}
    if grid is not None:
        kw['grid'] = grid
        kw['compiler_params'] = pltpu.CompilerParams(dimension_semantics=('arbitrary',) * len(grid),
                                                     vmem_limit_bytes=VMEM_LIMIT_BYTES)
    return pl.pallas_call(body, name=name, out_shape=out_shape, in_specs=in_specs, out_specs=out_specs,
                          scratch_shapes=list(scratch), **kw)


def _mm(a, b, mode, bias=None, out_dtype=F32, name='mm'):
    if mode == 'nn':
        (M, K), (K2, N) = a.shape, b.shape
    elif mode == 'nt':
        (M, K), (N, K2) = a.shape, b.shape
    else:
        (K, M), (K2, N) = a.shape, b.shape
    assert K == K2, (a.shape, b.shape, mode)
    tm = _tile(M, 1024, 128 if mode == 'tn' else 16)
    tn = _tile(N, 1408, 128)
    tk = _tile(K, 1024, 16 if mode == 'tn' else 128)
    nk = K // tk
    if mode == 'nn':
        a_spec = pl.BlockSpec((tm, tk), lambda i, j, k: (i, k))
        b_spec = pl.BlockSpec((tk, tn), lambda i, j, k: (k, j))
        dims = ((1,), (0,))
    elif mode == 'nt':
        a_spec = pl.BlockSpec((tm, tk), lambda i, j, k: (i, k))
        b_spec = pl.BlockSpec((tn, tk), lambda i, j, k: (j, k))
        dims = ((1,), (1,))
    else:
        a_spec = pl.BlockSpec((tk, tm), lambda i, j, k: (k, i))
        b_spec = pl.BlockSpec((tk, tn), lambda i, j, k: (k, j))
        dims = ((0,), (0,))
    has_bias = bias is not None

    def body(*refs):
        if has_bias:
            a_ref, b_ref, bias_ref, o_ref, acc = refs
        else:
            a_ref, b_ref, o_ref, acc = refs
        kk = pl.program_id(2)

        @pl.when(kk == 0)
        def _():
            acc[...] = jnp.zeros_like(acc)

        acc[...] += lax.dot_general(a_ref[...].astype(BF16), b_ref[...].astype(BF16), (dims, ((), ())),
                                    preferred_element_type=F32)

        @pl.when(kk == nk - 1)
        def _():
            r = acc[...]
            if has_bias:
                r = r + bias_ref[...]
            o_ref[...] = r.astype(out_dtype)

    in_specs = [a_spec, b_spec]
    args = [a, b]
    if has_bias:
        in_specs.append(pl.BlockSpec((1, tn), lambda i, j, k: (0, j)))
        args.append(bias.reshape(1, N).astype(F32))
    return _call(body, name=name, grid=(M // tm, N // tn, nk), in_specs=in_specs,
                 out_specs=pl.BlockSpec((tm, tn), lambda i, j, k: (i, j)),
                 out_shape=jax.ShapeDtypeStruct((M, N), out_dtype),
                 scratch=[pltpu.VMEM((tm, tn), F32)])(*args)


def _colsum(x, name='colsum'):
    M, N = x.shape
    tm, tn = _tile(M, 1024, 8), _tile(N, 1024, 128)

    def body(x_ref, o_ref):
        @pl.when(pl.program_id(1) == 0)
        def _():
            o_ref[...] = jnp.zeros_like(o_ref)

        o_ref[...] += jnp.sum(x_ref[...].astype(F32), axis=0, keepdims=True)

    return _call(body, name=name, grid=(N // tn, M // tm),
                 in_specs=[pl.BlockSpec((tm, tn), lambda j, i: (i, j))],
                 out_specs=pl.BlockSpec((1, tn), lambda j, i: (0, j)),
                 out_shape=jax.ShapeDtypeStruct((1, N), F32))(x)


@jax.custom_vjp
def linear(x, w, b):
    return _mm(x, w, 'nn', bias=b, name='lin_fwd')


def _linear_fwd(x, w, b):
    return _mm(x, w, 'nn', bias=b, name='lin_fwd'), (x, w)


def _linear_bwd(res, dy):
    x, w = res
    return (_mm(dy, w, 'nt', name='lin_dx'), _mm(x, dy, 'tn', out_dtype=w.dtype, name='lin_dw'),
            _colsum(dy, name='lin_db')[0])


linear.defvjp(_linear_fwd, _linear_bwd)


@jax.custom_vjp
def linear_nb(x, w):
    return _mm(x, w, 'nn', name='linnb_fwd')


def _linear_nb_fwd(x, w):
    return _mm(x, w, 'nn', name='linnb_fwd'), (x, w)


def _linear_nb_bwd(res, dy):
    x, w = res
    return _mm(dy, w, 'nt', name='linnb_dx'), _mm(x, dy, 'tn', out_dtype=w.dtype, name='linnb_dw')


linear_nb.defvjp(_linear_nb_fwd, _linear_nb_bwd)


class Geom:
    def __init__(self, B, L, Lc, D, depth):
        self.B, self.L, self.Lc, self.D, self.depth = B, L, Lc, D, depth
        self.TL = _tile(math.gcd(L, Lc), 256, 8)
        self.tpe, self.tpc = L // self.TL, Lc // self.TL
        self.R = B * (L + Lc)
        self.NT = self.R // self.TL
        self.n_lat_tiles = B * self.tpe
        self.alpha = (2 * depth) ** 0.25

    def pidx(self, i):
        return jnp.minimum(i // self.tpe, self.B)

    def first_of_group(self, i):
        return jnp.where(i < self.n_lat_tiles, (i % self.tpe) == 0, i == self.n_lat_tiles)

    def seqpos(self, i):
        lat = i < self.n_lat_tiles
        pos0 = jnp.where(lat, (i % self.tpe) * self.TL, ((i - self.n_lat_tiles) % self.tpc) * self.TL)
        return pos0, jnp.where(lat, self.L, self.Lc)


def _make_rowop(g, fn, nr, nc, npar, name):
    def specs(rows, consts, params):
        hm = rows[0].shape[0] // g.R
        rs = [pl.BlockSpec((g.TL * hm, a.shape[1]), lambda i: (i, 0)) for a in list(rows) + list(consts)]
        ps = [pl.BlockSpec((None,) + a.shape[1:], lambda i: (g.pidx(i), 0, 0)) for a in params]
        return hm, rs, ps

    def out_struct(rows, consts, params, hm):
        tiles = [jax.ShapeDtypeStruct((g.TL * hm, a.shape[1]), F32) for a in list(rows) + list(consts)]
        tiles += [jax.ShapeDtypeStruct(a.shape[1:], F32) for a in params]
        return jax.eval_shape(fn, *tiles)

    def run_fwd(rows, consts, params):
        hm, rs, ps = specs(rows, consts, params)
        outs = out_struct(rows, consts, params, hm)
        n_in = nr + nc + npar

        def body(*refs):
            vals = [r[...] for r in refs[:n_in]]
            res = fn(*vals)
            for o_ref, v in zip(refs[n_in:], res):
                o_ref[...] = v

        return _call(body, name=name + '_fwd', grid=(g.NT,), in_specs=rs + ps,
                     out_specs=[pl.BlockSpec(o.shape, lambda i: (i, 0)) for o in outs],
                     out_shape=[jax.ShapeDtypeStruct((g.R * hm, o.shape[1]), F32) for o in outs])(
                         *rows, *consts, *params)

    def run_bwd(rows, consts, params, douts):
        hm, rs, ps = specs(rows, consts, params)
        outs = out_struct(rows, consts, params, hm)
        n_in = nr + nc + npar
        n_do = len(outs)

        def body(*refs):
            rv = [r[...] for r in refs[:nr]]
            cv = [r[...] for r in refs[nr:nr + nc]]
            pv = [r[...] for r in refs[nr + nc:n_in]]
            dov = tuple(r[...] for r in refs[n_in:n_in + n_do])
            drow_refs = refs[n_in + n_do:n_in + n_do + nr]
            dpar_refs = refs[n_in + n_do + nr:]

            def f(*d):
                return fn(*d[:nr], *cv, *d[nr:])

            _, vjp = jax.vjp(f, *rv, *pv)
            gr = vjp(dov)
            for r, v in zip(drow_refs, gr[:nr]):
                r[...] = v
            if npar:
                @pl.when(g.first_of_group(pl.program_id(0)))
                def _():
                    for r in dpar_refs:
                        r[...] = jnp.zeros_like(r)

                for r, v in zip(dpar_refs, gr[nr:]):
                    r[...] += v

        do_specs = [pl.BlockSpec(o.shape, lambda i: (i, 0)) for o in outs]
        res = _call(body, name=name + '_bwd', grid=(g.NT,), in_specs=rs + ps + do_specs,
                    out_specs=rs[:nr] + ps,
                    out_shape=[jax.ShapeDtypeStruct(a.shape, F32) for a in list(rows) + list(params)])(
                        *rows, *consts, *params, *douts)
        return tuple(res[:nr]), tuple(res[nr:])

    @jax.custom_vjp
    def op(rows, consts, params):
        return tuple(run_fwd(rows, consts, params))

    def op_fwd(rows, consts, params):
        return tuple(run_fwd(rows, consts, params)), (rows, consts, params)

    def op_bwd(res, douts):
        rows, consts, params = res
        drows, dpars = run_bwd(rows, consts, params, douts)
        return drows, tuple(jnp.zeros_like(c) for c in consts), dpars

    op.defvjp(op_fwd, op_bwd)
    return op


def _lane_iota(shape):
    return lax.broadcasted_iota(jnp.int32, shape, len(shape) - 1)


@jax.custom_vjp
def _swap_halves(t):
    w = t.shape[-1]
    first = (_lane_iota(t.shape) % 64) < 32
    return jnp.where(first, pltpu.roll(t, w - 32, t.ndim - 1), pltpu.roll(t, 32, t.ndim - 1))


_swap_halves.defvjp(lambda t: (_swap_halves(t), None), lambda _, ct: (_swap_halves(ct),))


def _ln(t, gain, bias):
    mu = jnp.mean(t, axis=-1, keepdims=True)
    d = t - mu
    var = jnp.mean(d * d, axis=-1, keepdims=True)
    return d * lax.rsqrt(var + LN_EPS) * gain + bias


def _sig(t):
    return jax.nn.sigmoid(t)


def _build_rowops(g):
    alpha = g.alpha

    def f_mod(x, shift, scale):
        return (x * (1.0 + scale) + shift,)

    def f_qk(z, cos, sin, gain):
        ms = jnp.mean(z * z, axis=-1, keepdims=True)
        t = z * lax.rsqrt(ms + RMS_EPS) * gain
        return (t * cos + _swap_halves(t) * sin,)

    def f_glu(a, gt):
        return (a * _sig(gt),)

    def f_lnsilu(c1, gain, bias):
        y = _ln(c1, gain, bias)
        return (y * _sig(y),)

    def f_merge(g0, g1, g2, at, co, mx, ps):
        return (_sig(g0) * at + _sig(g1) * co + _sig(g2) * (mx * ps),)

    def f_resln(x, y, gate, gain, bias):
        return (_ln(alpha * x + gate * y, gain, bias),)

    def f_silumul(a, u):
        return (a * _sig(a) * u,)

    mk = functools.partial(_make_rowop, g)
    return dict(mod=mk(f_mod, 1, 0, 2, 'mod'), qk=mk(f_qk, 1, 2, 1, 'qk'), glu=mk(f_glu, 2, 0, 0, 'glu'),
                lnsilu=mk(f_lnsilu, 1, 0, 2, 'lnsilu'), merge=mk(f_merge, 6, 0, 1, 'merge'),
                resln=mk(f_resln, 2, 0, 3, 'resln'), silumul=mk(f_silumul, 2, 0, 0, 'silumul'))


def _shifted(xp, xc, xn, off, t, TL):
    if off == 0:
        return xc
    s = (-off) % TL
    rc = pltpu.roll(xc, s, 0)
    if off < 0:
        return jnp.where(t + off < 0, pltpu.roll(xp, s, 0), rc)
    return jnp.where(t + off >= TL, pltpu.roll(xn, s, 0), rc)


def _in_seq(v, pos, ls):
    return jnp.where(pos >= 0, jnp.where(pos < ls, v, 0.0), 0.0)


def _halo_specs(g, tc):
    nt = g.NT
    return [pl.BlockSpec((g.TL, tc), lambda i, j: (jnp.maximum(i - 1, 0), j)),
            pl.BlockSpec((g.TL, tc), lambda i, j: (i, j)),
            pl.BlockSpec((g.TL, tc), lambda i, j: (jnp.minimum(i + 1, nt - 1), j))]


def _conv_call(g, x, w, b, name):
    K, C = w.shape
    p = (K - 1) // 2
    tc = _tile(C, 512, 128)
    TL = g.TL

    def body(xp_ref, xc_ref, xn_ref, w_ref, b_ref, o_ref):
        pos0, ls = g.seqpos(pl.program_id(0))
        xp, xc, xn = xp_ref[...], xc_ref[...], xn_ref[...]
        t = lax.broadcasted_iota(jnp.int32, (TL, tc), 0)
        acc = jnp.zeros((TL, tc), F32) + b_ref[...]
        for k in range(K):
            off = k - p
            acc = acc + _in_seq(_shifted(xp, xc, xn, off, t, TL), pos0 + t + off, ls) * w_ref[pl.ds(k, 1), :]
        o_ref[...] = acc

    return _call(body, name=name, grid=(g.NT, C // tc),
                 in_specs=_halo_specs(g, tc) + [pl.BlockSpec((K, tc), lambda i, j: (0, j)),
                                                pl.BlockSpec((1, tc), lambda i, j: (0, j))],
                 out_specs=pl.BlockSpec((TL, tc), lambda i, j: (i, j)),
                 out_shape=jax.ShapeDtypeStruct(x.shape, F32))(x, x, x, w, b.reshape(1, C))


def _conv_dw_call(g, x, dy, K, name):
    C = x.shape[1]
    p = (K - 1) // 2
    tc = _tile(C, 512, 128)
    TL = g.TL

    def body(xp_ref, xc_ref, xn_ref, dy_ref, dw_ref, db_ref):
        i = pl.program_id(1)
        pos0, ls = g.seqpos(i)

        @pl.when(i == 0)
        def _():
            dw_ref[...] = jnp.zeros_like(dw_ref)
            db_ref[...] = jnp.zeros_like(db_ref)

        xp, xc, xn, dy_v = xp_ref[...], xc_ref[...], xn_ref[...], dy_ref[...]
        t = lax.broadcasted_iota(jnp.int32, (TL, tc), 0)
        db_ref[...] += jnp.sum(dy_v, axis=0, keepdims=True)
        for k in range(K):
            off = k - p
            sl = _in_seq(_shifted(xp, xc, xn, off, t, TL), pos0 + t + off, ls)
            dw_ref[pl.ds(k, 1), :] += jnp.sum(sl * dy_v, axis=0, keepdims=True)

    nt = g.NT
    halo = [pl.BlockSpec((TL, tc), lambda j, i: (jnp.maximum(i - 1, 0), j)),
            pl.BlockSpec((TL, tc), lambda j, i: (i, j)),
            pl.BlockSpec((TL, tc), lambda j, i: (jnp.minimum(i + 1, nt - 1), j))]
    return _call(body, name=name, grid=(C // tc, g.NT),
                 in_specs=halo + [pl.BlockSpec((TL, tc), lambda j, i: (i, j))],
                 out_specs=[pl.BlockSpec((K, tc), lambda j, i: (0, j)), pl.BlockSpec((1, tc), lambda j, i: (0, j))],
                 out_shape=[jax.ShapeDtypeStruct((K, C), F32), jax.ShapeDtypeStruct((1, C), F32)])(x, x, x, dy)


def _pool_call(g, u, transpose, name):
    C = u.shape[1]
    cg = C // len(POOL_WINDOWS)
    TL = g.TL

    def body(up_ref, uc_ref, un_ref, o_ref):
        pos0, ls = g.seqpos(pl.program_id(0))
        grp = pl.program_id(1)
        t = lax.broadcasted_iota(jnp.int32, (TL, cg), 0)
        for gi, wdw in enumerate(POOL_WINDOWS):
            @pl.when(grp == gi)
            def _(wdw=wdw):
                half = wdw // 2

                def cnt(pos):
                    c = jnp.minimum(pos - half + wdw, ls) - jnp.maximum(pos - half, 0)
                    return jnp.maximum(c, 1).astype(F32)

                up, uc, un = up_ref[...], uc_ref[...], un_ref[...]
                if transpose:
                    up, uc, un = up / cnt(pos0 + t - TL), uc / cnt(pos0 + t), un / cnt(pos0 + t + TL)
                    offs = range(-half + 1, half + 1)
                else:
                    offs = range(-half, half)
                acc = jnp.zeros((TL, cg), F32)
                for off in offs:
                    acc = acc + _in_seq(_shifted(up, uc, un, off, t, TL), pos0 + t + off, ls)
                if transpose:
                    o_ref[...] = acc - uc_ref[...]
                else:
                    o_ref[...] = acc / cnt(pos0 + t) - uc

    return _call(body, name=name, grid=(g.NT, len(POOL_WINDOWS)), in_specs=_halo_specs(g, cg),
                 out_specs=pl.BlockSpec((TL, cg), lambda i, j: (i, j)),
                 out_shape=jax.ShapeDtypeStruct(u.shape, F32))(u, u, u)


def _build_seqops(g):
    @jax.custom_vjp
    def dwconv(x, w, b):
        return _conv_call(g, x, w, b, 'conv_fwd')

    def dwconv_fwd(x, w, b):
        return _conv_call(g, x, w, b, 'conv_fwd'), (x, w)

    def dwconv_bwd(res, dy):
        x, w = res
        dx = _conv_call(g, dy, w[::-1], jnp.zeros((w.shape[1],), F32), 'conv_dx')
        dw, db = _conv_dw_call(g, x, dy, w.shape[0], 'conv_dw')
        return dx, dw, db[0]

    dwconv.defvjp(dwconv_fwd, dwconv_bwd)

    @jax.custom_vjp
    def pool(u):
        return _pool_call(g, u, False, 'pool_fwd')

    pool.defvjp(lambda u: (_pool_call(g, u, False, 'pool_fwd'), None),
                lambda _, dp: (_pool_call(g, dp, True, 'pool_bwd'),))
    return dwconv, pool


def _attn_probs(qg, ks, scale):
    ss = [lax.dot_general(qg, k, (((1,), (1,)), ((), ())), preferred_element_type=F32) * scale for k in ks]
    m = ss[0].max(axis=-1, keepdims=True)
    for s in ss[1:]:
        m = jnp.maximum(m, s.max(axis=-1, keepdims=True))
    es = [jnp.exp(s - m) for s in ss]
    den = es[0].sum(axis=-1, keepdims=True)
    for e in es[1:]:
        den = den + e.sum(axis=-1, keepdims=True)
    inv = 1.0 / den
    return [e * inv for e in es]


def _attn_specs(B, q, kvs):
    lq = q.shape[0] // B
    kvh = kvs[0].shape[1] // HEAD_DIM
    grp = q.shape[1] // HEAD_DIM // kvh
    tq = _tile(lq, 256, 8)
    nq = lq // tq
    q_spec = pl.BlockSpec((tq, grp * HEAD_DIM), lambda b, h, i: (b * nq + i, h))
    kv_specs = [pl.BlockSpec((a.shape[0] // B, HEAD_DIM), lambda b, h, i: (b, h)) for a in kvs]
    return kvh, grp, tq, nq, q_spec, kv_specs


def _attn_fwd_call(B, q, kvs, name):
    kvh, grp, tq, nq, q_spec, kv_specs = _attn_specs(B, q, kvs)
    ns = len(kvs) // 2
    scale = HEAD_DIM ** -0.5

    def body(*refs):
        q_ref, o_ref = refs[0], refs[-1]
        ks = [refs[1 + 2 * s][...].astype(BF16) for s in range(ns)]
        vs = [refs[2 + 2 * s][...].astype(BF16) for s in range(ns)]
        for gi in range(grp):
            sl = slice(gi * HEAD_DIM, (gi + 1) * HEAD_DIM)
            ps = _attn_probs(q_ref[:, sl].astype(BF16), ks, scale)
            o = jnp.zeros((tq, HEAD_DIM), F32)
            for p, v in zip(ps, vs):
                o = o + jnp.dot(p.astype(BF16), v, preferred_element_type=F32)
            o_ref[:, sl] = o

    return _call(body, name=name, grid=(B, kvh, nq), in_specs=[q_spec] + kv_specs, out_specs=q_spec,
                 out_shape=jax.ShapeDtypeStruct(q.shape, F32))(q, *kvs)


def _attn_bwd_call(B, q, kvs, do, name):
    kvh, grp, tq, nq, q_spec, kv_specs = _attn_specs(B, q, kvs)
    ns = len(kvs) // 2
    scale = HEAD_DIM ** -0.5

    def body(*refs):
        q_ref, do_ref = refs[0], refs[1 + 2 * ns]
        dq_ref = refs[2 + 2 * ns]
        dkv_refs = refs[3 + 2 * ns:]
        ks = [refs[1 + 2 * s][...].astype(BF16) for s in range(ns)]
        vs = [refs[2 + 2 * s][...].astype(BF16) for s in range(ns)]

        @pl.when(pl.program_id(2) == 0)
        def _():
            for r in dkv_refs:
                r[...] = jnp.zeros_like(r)

        for gi in range(grp):
            sl = slice(gi * HEAD_DIM, (gi + 1) * HEAD_DIM)
            qg = q_ref[:, sl].astype(BF16)
            dog = do_ref[:, sl].astype(BF16)
            ps = _attn_probs(qg, ks, scale)
            dps = [lax.dot_general(dog, v, (((1,), (1,)), ((), ())), preferred_element_type=F32) for v in vs]
            delta = (ps[0] * dps[0]).sum(axis=-1, keepdims=True)
            for p, dp in zip(ps[1:], dps[1:]):
                delta = delta + (p * dp).sum(axis=-1, keepdims=True)
            dq = jnp.zeros((tq, HEAD_DIM), F32)
            for s in range(ns):
                ds = (ps[s] * (dps[s] - delta) * scale).astype(BF16)
                dq = dq + jnp.dot(ds, ks[s], preferred_element_type=F32)
                dkv_refs[2 * s][...] += lax.dot_general(ds, qg, (((0,), (0,)), ((), ())),
                                                        preferred_element_type=F32)
                dkv_refs[2 * s + 1][...] += lax.dot_general(ps[s].astype(BF16), dog, (((0,), (0,)), ((), ())),
                                                            preferred_element_type=F32)
            dq_ref[:, sl] = dq

    res = _call(body, name=name, grid=(B, kvh, nq), in_specs=[q_spec] + kv_specs + [q_spec],
                out_specs=[q_spec] + kv_specs,
                out_shape=[jax.ShapeDtypeStruct(a.shape, F32) for a in [q] + list(kvs)])(q, *kvs, do)
    return tuple(res)


def _build_attn(B):
    @jax.custom_vjp
    def attn(q, kvs):
        return _attn_fwd_call(B, q, kvs, 'attn%d_fwd' % (len(kvs) // 2))

    def attn_fwd(q, kvs):
        return attn(q, kvs), (q, kvs)

    def attn_bwd(res, do):
        q, kvs = res
        out = _attn_bwd_call(B, q, kvs, do, 'attn%d_bwd' % (len(kvs) // 2))
        return out[0], tuple(out[1:])

    attn.defvjp(attn_fwd, attn_bwd)
    return attn


def _ew(fn, arrays, out_dtypes, name):
    rows, cols = arrays[0].shape
    tr = _tile(rows, 256, 16)
    n_in = len(arrays)

    def body(*refs):
        res = fn(*[r[...] for r in refs[:n_in]])
        for r, v in zip(refs[n_in:], res):
            r[...] = v.astype(r.dtype)

    spec = pl.BlockSpec((tr, cols), lambda i: (i, 0))
    return _call(body, name=name, grid=(rows // tr,), in_specs=[spec] * n_in, out_specs=[spec] * len(out_dtypes),
                 out_shape=[jax.ShapeDtypeStruct((rows, cols), d) for d in out_dtypes])(*arrays)


def _as2d(a):
    return a.reshape(-1, a.shape[-1]) if a.ndim > 1 else a.reshape(1, -1)


def _sum_slots(x, name):
    n, rows, cols = x.shape
    tr = _tile(rows, 256, 16)

    def body(x_ref, o_ref):
        acc = x_ref[0].astype(F32)
        for s in range(1, n):
            acc = acc + x_ref[s].astype(F32)
        o_ref[...] = acc

    return _call(body, name=name, grid=(rows // tr,), in_specs=[pl.BlockSpec((n, tr, cols), lambda i: (0, i, 0))],
                 out_specs=pl.BlockSpec((tr, cols), lambda i: (i, 0)),
                 out_shape=jax.ShapeDtypeStruct((rows, cols), F32))(x)


def _adamw(w, gr, m, v, name):
    b1c = 1.0 - ADAM_B1 ** ADAM_STEP
    b2c = 1.0 - ADAM_B2 ** ADAM_STEP

    def fn(w_, g_, m_, v_):
        mn = ADAM_B1 * m_ + (1.0 - ADAM_B1) * g_
        vn = ADAM_B2 * v_ + (1.0 - ADAM_B2) * (g_ * g_)
        delta = -ADAM_LR * ((mn / b1c) / (jnp.sqrt(vn / b2c) + ADAM_EPS) + ADAM_WD * w_)
        return delta, mn, vn

    shp = w.shape
    outs = _ew(fn, [_as2d(a) for a in (w, gr, m, v)], [F32, F32, F32], name)
    return [o.reshape(shp) for o in outs]


def _loss_call(y, target):
    rows, D = y.shape
    tr = _tile(rows, 512, 8)

    def body(y_ref, t_ref, dy_ref, l_ref):
        @pl.when(pl.program_id(0) == 0)
        def _():
            l_ref[...] = jnp.zeros_like(l_ref)

        e = y_ref[...] - t_ref[...]
        dy_ref[...] = e * (1.0 / D)
        s = jnp.sum(jnp.sum(e * e, axis=1, keepdims=True), axis=0, keepdims=True)
        l_ref[...] += s * (0.5 / D)

    spec = pl.BlockSpec((tr, D), lambda i: (i, 0))
    dy, lsum = _call(body, name='loss', grid=(rows // tr,), in_specs=[spec, spec],
                     out_specs=[spec, pl.BlockSpec((1, 128), lambda i: (0, 0))],
                     out_shape=[jax.ShapeDtypeStruct((rows, D), F32), jax.ShapeDtypeStruct((1, 128), F32)])(y, target)
    return dy, lsum[0, 0]


ANY = pl.BlockSpec(memory_space=pl.ANY)


def _coords():
    return lax.axis_index('x'), lax.axis_index('y'), lax.axis_index('c')


def _gather_chips(arrs, name):
    n = len(arrs)
    rel = [(1, 0), (0, 1), (1, 1)]

    def body(*refs):
        ins, outs = refs[:n], refs[n:2 * n]
        send, recv, loc = refs[2 * n:]
        x, y, c = _coords()
        me = 2 * x + y
        local = [pltpu.make_async_copy(ins[a], outs[a].at[me], loc.at[a]) for a in range(n)]
        for cp in local:
            cp.start()
        remote = []
        for a in range(n):
            for k, (dx, dy) in enumerate(rel):
                cp = pltpu.make_async_remote_copy(src_ref=ins[a], dst_ref=outs[a].at[me], send_sem=send.at[a * 3 + k],
                                                  recv_sem=recv.at[a * 3 + k],
                                                  device_id=((x + dx) % 2, (y + dy) % 2, c), device_id_type=MESH)
                cp.start()
                remote.append(cp)
        for cp in remote:
            cp.wait()
        for cp in local:
            cp.wait()

    return pl.pallas_call(body, name=name, in_specs=[ANY] * n, out_specs=[ANY] * n,
                          out_shape=[jax.ShapeDtypeStruct((N_CHIPS,) + a.shape, a.dtype) for a in arrs],
                          scratch_shapes=[pltpu.SemaphoreType.DMA((3 * n,)), pltpu.SemaphoreType.DMA((3 * n,)),
                                          pltpu.SemaphoreType.DMA((n,))])(*arrs)


def _gather_all(arr, name):
    def body(in_ref, out_ref, send, recv, loc):
        x, y, c = _coords()
        me = 4 * x + 2 * y + c
        local = pltpu.make_async_copy(in_ref, out_ref.at[me], loc.at[0])
        local.start()
        remote = []
        for k in range(1, N_DEV):
            dx, dy, dc = (k >> 2) & 1, (k >> 1) & 1, k & 1
            cp = pltpu.make_async_remote_copy(src_ref=in_ref, dst_ref=out_ref.at[me], send_sem=send.at[k - 1],
                                              recv_sem=recv.at[k - 1],
                                              device_id=((x + dx) % 2, (y + dy) % 2, (c + dc) % 2), device_id_type=MESH)
            cp.start()
            remote.append(cp)
        for cp in remote:
            cp.wait()
        local.wait()

    return pl.pallas_call(body, name=name, in_specs=[ANY], out_specs=ANY,
                          out_shape=jax.ShapeDtypeStruct((N_DEV,) + arr.shape, arr.dtype),
                          scratch_shapes=[pltpu.SemaphoreType.DMA((N_DEV - 1,)), pltpu.SemaphoreType.DMA((N_DEV - 1,)),
                                          pltpu.SemaphoreType.DMA((1,))])(arr)


def _exchange_shards(arrs, name):
    n = len(arrs)
    rel = [(1, 0), (0, 1), (1, 1)]

    def body(*refs):
        ins, outs = refs[:n], refs[n:2 * n]
        send, recv, loc, sib_send, sib_recv = refs[2 * n:]
        x, y, c = _coords()
        me = 2 * x + y
        local = [pltpu.make_async_copy(ins[a].at[me], outs[a].at[c, me], loc.at[a]) for a in range(n)]
        for cp in local:
            cp.start()
        remote = []
        for a in range(n):
            for k, (dx, dy) in enumerate(rel):
                tx, ty = (x + dx) % 2, (y + dy) % 2
                cp = pltpu.make_async_remote_copy(src_ref=ins[a].at[2 * tx + ty], dst_ref=outs[a].at[c, me],
                                                  send_sem=send.at[a * 3 + k], recv_sem=recv.at[a * 3 + k],
                                                  device_id=(tx, ty, c), device_id_type=MESH)
                cp.start()
                remote.append(cp)
        for cp in remote:
            cp.wait()
        for cp in local:
            cp.wait()
        sib = [pltpu.make_async_remote_copy(src_ref=outs[a].at[c], dst_ref=outs[a].at[c], send_sem=sib_send.at[a],
                                            recv_sem=sib_recv.at[a], device_id=(x, y, 1 - c), device_id_type=MESH)
               for a in range(n)]
        for cp in sib:
            cp.start()
        for a in range(n):
            sib[a].wait_send()
            pltpu.make_async_remote_copy(src_ref=outs[a].at[1 - c], dst_ref=outs[a].at[1 - c], send_sem=sib_send.at[a],
                                         recv_sem=sib_recv.at[a], device_id=(x, y, 1 - c),
                                         device_id_type=MESH).wait_recv()

    return pl.pallas_call(body, name=name, in_specs=[ANY] * n, out_specs=[ANY] * n,
                          out_shape=[jax.ShapeDtypeStruct((2,) + a.shape, a.dtype) for a in arrs],
                          scratch_shapes=[pltpu.SemaphoreType.DMA((3 * n,)), pltpu.SemaphoreType.DMA((3 * n,)),
                                          pltpu.SemaphoreType.DMA((n,)), pltpu.SemaphoreType.DMA((n,)),
                                          pltpu.SemaphoreType.DMA((n,))])(*arrs)


def _rope_tables(g, heads):
    t = jnp.arange(g.L, dtype=F32)
    row, col = jnp.floor(t / GRID_W), t % GRID_W
    inv = ROPE_THETA ** (-jnp.arange(ROPE_PAIRS, dtype=F32) / ROPE_PAIRS)
    ang = jnp.stack([row, col], axis=-1)[..., None] * inv
    cos = jnp.broadcast_to(jnp.cos(ang)[:, :, None, :], (g.L, 2, 2, ROPE_PAIRS)).reshape(g.L, HEAD_DIM)
    sin = jnp.sin(ang)[:, :, None, :] * jnp.array([-1.0, 1.0], F32)[None, None, :, None]
    sin = sin.reshape(g.L, HEAD_DIM)
    n_ctx = g.B * g.Lc
    cos = jnp.concatenate([jnp.tile(cos, (g.B, 1)), jnp.ones((n_ctx, HEAD_DIM), F32)])
    sin = jnp.concatenate([jnp.tile(sin, (g.B, 1)), jnp.zeros((n_ctx, HEAD_DIM), F32)])
    return jnp.repeat(cos, heads, axis=0), jnp.repeat(sin, heads, axis=0)


def _forward(g, ops, big, small, mods, x, ctx):
    D, B = g.D, g.B
    rop, dwconv, pool, attn = ops
    nh = D // HEAD_DIM
    nkv = nh // 4
    kvw = nkv * HEAD_DIM
    q_end, k_end, v_end = D, D + kvw, D + 2 * kvw
    conv_end = v_end + 2 * D
    pool_end = conv_end + D
    cg = D // len(POOL_WINDOWS)
    n_lat = B * g.L
    cos_q, sin_q = _rope_tables(g, nh)
    cos_k, sin_k = _rope_tables(g, nkv)
    xa = jnp.concatenate([x.reshape(n_lat, D), ctx.reshape(B * g.Lc, D)])

    def shared(v):
        return jnp.broadcast_to(v[None, None, :], (B + 1, 1, v.shape[0]))

    for l in range(g.depth):
        md = [mods[l, :, j][:, None, :] for j in range(6)]
        w_in, b_in = big['w_in'][l], small['b_in'][l]

        def proj(h, a, b):
            return linear(h, w_in[:, a:b], b_in[a:b])

        (h,) = rop['mod']((xa,), (), (md[0], md[1]))
        zq, zk, zv = proj(h, 0, q_end), proj(h, q_end, k_end), proj(h, k_end, v_end)
        (qh,) = rop['qk']((zq.reshape(-1, HEAD_DIM),), (cos_q, sin_q), (shared(small['q_gain'][l]),))
        (kh,) = rop['qk']((zk.reshape(-1, HEAD_DIM),), (cos_k, sin_k), (shared(small['k_gain'][l]),))
        qh, kh = qh.reshape(-1, D), kh.reshape(-1, kvw)
        o_lat = attn(qh[:n_lat], (kh[n_lat:], zv[n_lat:], kh[:n_lat], zv[:n_lat]))
        o_ctx = attn(qh[n_lat:], (kh[n_lat:], zv[n_lat:]))
        attn_o = jnp.concatenate([o_lat, o_ctx])

        (glu,) = rop['glu']((proj(h, v_end, v_end + D), proj(h, v_end + D, conv_end)), (), ())
        c1 = dwconv(glu, small['conv_dw_w'][l], small['conv_dw_b'][l])
        (c2,) = rop['lnsilu']((c1,), (), (shared(small['conv_ln_g'][l]), shared(small['conv_ln_b'][l])))
        conv_o = linear(c2, big['conv_pw_w'][l], small['conv_pw_b'][l])

        pooled = pool(proj(h, conv_end, pool_end))
        w_bd = jnp.zeros((D, D), BF16)
        for gi in range(len(POOL_WINDOWS)):
            w_bd = lax.dynamic_update_slice(w_bd, big['pool_w'][l, gi], (gi * cg, gi * cg))
        mixed = linear_nb(pooled, w_bd)

        gates = [proj(h, pool_end + j * D, pool_end + (j + 1) * D) for j in range(3)]
        (m,) = rop['merge']((gates[0], gates[1], gates[2], attn_o, conv_o, mixed), (),
                            (shared(small['pool_scale'][l]),))
        y = linear(m, big['w_out'][l], small['b_out'][l])
        (x1,) = rop['resln']((xa, y), (), (md[2], shared(small['ln1_g'][l]), shared(small['ln1_b'][l])))

        (h2,) = rop['mod']((x1,), (), (md[3], md[4]))
        dff = big['w_up'].shape[2] // 2
        ua = linear_nb(h2, big['w_up'][l][:, :dff])
        uu = linear_nb(h2, big['w_up'][l][:, dff:])
        a = dwconv(ua, small['ffn_dw_w'][l], small['ffn_dw_b'][l])
        (f,) = rop['silumul']((a, uu), (), ())
        y2 = linear_nb(f, big['w_down'][l])
        (xa,) = rop['resln']((x1, y2), (), (md[5], shared(small['ln2_g'][l]), shared(small['ln2_b'][l])))
    return xa[:n_lat]


def _silu(z):
    return (z * _sig(z),)


def _dsilu(z, d):
    s = _sig(z)
    return (d * s * (1.0 + z * (1.0 - s)),)


def kernel(x, c, ctx, c_ctx, w_ada, b_ada, w_in, b_in, q_gain, k_gain, conv_dw_w, conv_dw_b, conv_ln_g, conv_ln_b, conv_pw_w, conv_pw_b, pool_w, pool_scale, w_out, b_out, ln1_g, ln1_b, ln2_g, ln2_b, w_up, ffn_dw_w, ffn_dw_b, w_down, loss_target, m_c_ctx, m_w_ada, m_b_ada, m_w_in, m_b_in, m_q_gain, m_k_gain, m_conv_dw_w, m_conv_dw_b, m_conv_ln_g, m_conv_ln_b, m_conv_pw_w, m_conv_pw_b, m_pool_w, m_pool_scale, m_w_out, m_b_out, m_ln1_g, m_ln1_b, m_ln2_g, m_ln2_b, m_w_up, m_ffn_dw_w, m_ffn_dw_b, m_w_down, v_c_ctx, v_w_ada, v_b_ada, v_w_in, v_b_in, v_q_gain, v_k_gain, v_conv_dw_w, v_conv_dw_b, v_conv_ln_g, v_conv_ln_b, v_conv_pw_w, v_conv_pw_b, v_pool_w, v_pool_scale, v_w_out, v_b_out, v_ln1_g, v_ln1_b, v_ln2_g, v_ln2_b, v_w_up, v_ffn_dw_w, v_ffn_dw_b, v_w_down):
    A = dict(locals())
    B, L, D = x.shape
    Lc = ctx.shape[1]
    depth = w_in.shape[0]
    g = Geom(B, L, Lc, D, depth)
    xi, yi, ci = _coords()
    chip = 2 * xi + yi
    dev = 4 * xi + 2 * yi + ci
    nrow = B + 1

    bf = {n: _ew(lambda t: (t,), [_as2d(A[n])], [BF16], 'cast_' + n)[0].reshape(A[n].shape) for n in BIG}
    names = list(BIG) + list(SMALL_SHARDED)
    gathered = _gather_chips([bf[n] for n in BIG] + [A[n] for n in SMALL_SHARDED], 'gather_weights')
    axis_of = {**BIG, **SMALL_SHARDED}
    full = {n: jnp.concatenate([gt[k] for k in range(N_CHIPS)], axis=axis_of[n]) for n, gt in zip(names, gathered)}

    crow = jnp.concatenate([c, c_ctx[None]], axis=0)
    c_all = _gather_all(crow, 'gather_cond').reshape(N_DEV * nrow, D)
    (a_all,) = _ew(_silu, [c_all], [F32], 'silu_cond')
    ncol = w_ada.shape[2]
    b_ada_sh = lax.dynamic_slice(b_ada, (0, chip * ncol), (depth, ncol))
    m_sh = jnp.stack([_mm(a_all, w_ada[l], 'nn', bias=b_ada_sh[l], name='ada_fwd') for l in range(depth)])
    m_full = _gather_chips([m_sh], 'gather_mods')[0]
    m_full = jnp.concatenate([m_full[k] for k in range(N_CHIPS)], axis=-1)
    mods = lax.dynamic_slice(m_full, (0, dev * nrow, 0), (depth, nrow, 6 * D)).reshape(depth, nrow, 6, D)

    ops = (_build_rowops(g), *_build_seqops(g), _build_attn(B))
    big = {n: full[n] for n in BIG}
    small = {n: (full[n] if n in SMALL_SHARDED else A[n]) for n in SMALL}

    y, vjp = jax.vjp(lambda bg, sm, md, xx: _forward(g, ops, bg, sm, md, xx, ctx), big, small, mods, x)
    dy, loss_part = _loss_call(y, loss_target.reshape(B * L, D))
    loss = lax.psum(loss_part, ('x', 'y', 'c'))
    d_big, d_small, d_mods, grad_x = vjp(dy)

    parts = [jnp.stack(jnp.split(d_big[n], N_CHIPS, axis=BIG[n]), axis=0) for n in BIG]
    recv = _exchange_shards(parts, 'exchange_grads')
    grads = {}
    for n, r in zip(BIG, recv):
        shp = A[n].shape
        s8 = r.reshape((2 * N_CHIPS, -1, shp[-1]))
        grads[n] = _sum_slots(s8, 'sum_' + n).reshape(shp)

    dm_all = _gather_all(d_mods.transpose(1, 0, 2, 3).reshape(nrow, depth * 6 * D), 'gather_dmods')
    dm_all = dm_all.reshape(N_DEV * nrow, depth, 6 * D)
    g_w_ada, g_b_ada, da = [], [], jnp.zeros((N_DEV * nrow, D), F32)
    for l in range(depth):
        dml = dm_all[:, l]
        dml_sh = lax.dynamic_slice(dml, (0, chip * ncol), (N_DEV * nrow, ncol))
        g_w_ada.append(_mm(a_all, dml_sh, 'tn', name='ada_dw'))
        g_b_ada.append(_colsum(dml, name='ada_db')[0])
        da = da + _mm(dml_sh, w_ada[l], 'nt', name='ada_da')
    grads['w_ada'] = jnp.stack(g_w_ada)
    grads['b_ada'] = jnp.stack(g_b_ada)
    (dc_all,) = _ew(_dsilu, [c_all, da], [F32], 'dsilu_cond')
    dc_ctx = dc_all.reshape(N_DEV, nrow, D)[:, B]
    dc_ctx = _sum_slots(dc_ctx.reshape(N_DEV, 1, D), 'sum_dcctx')
    dc_ctx = _gather_chips([jnp.broadcast_to(dc_ctx, (8, D))], 'gather_dcctx')[0]
    grads['c_ctx'] = _sum_slots(dc_ctx, 'sum_dcctx_chips')[0]

    flat = jnp.concatenate([d_small[n].reshape(-1) for n in SMALL])
    n_flat = flat.shape[0]
    pad = (-n_flat) % (128 * 16)
    flat = jnp.pad(flat, (0, pad)).reshape(-1, 128)
    tot = _sum_slots(_gather_all(flat, 'gather_small'), 'sum_small').reshape(-1)
    off = 0
    for n in SMALL:
        shp = d_small[n].shape
        sz = math.prod(shp)
        gn = tot[off:off + sz].reshape(shp)
        off += sz
        if n in SMALL_SHARDED:
            w = A[n].shape[SMALL_SHARDED[n]]
            gn = lax.dynamic_slice_in_dim(gn, chip * w, w, axis=SMALL_SHARDED[n])
        grads[n] = gn

    deltas, new_m, new_v = {}, {}, {}
    for n in W_NAMES:
        deltas[n], new_m[n], new_v[n] = _adamw(A[n], grads[n], A['m_' + n], A['v_' + n], 'adamw_' + n)
    return (loss, grad_x, *[grads[n] for n in W_NAMES], *[deltas[n] for n in W_NAMES],
            *[new_m[n] for n in W_NAMES], *[new_v[n] for n in W_NAMES])
```

```python
import functools
import math

import jax
import jax.numpy as jnp
from jax import lax
from jax.experimental import pallas as pl
from jax.experimental.pallas import tpu as pltpu

F32, BF16 = jnp.float32, jnp.bfloat16
MESH = pl.DeviceIdType.MESH
VMEM_LIMIT_BYTES = 48 * 1024 * 1024
GRID_W = 64
HEAD_DIM = 128
ROPE_THETA = 10000.0
ROPE_PAIRS = HEAD_DIM // 4
POOL_WINDOWS = (2, 4, 8, 16)
LN_EPS = 1e-5
RMS_EPS = 1e-6
ADAM_LR, ADAM_B1, ADAM_B2, ADAM_EPS, ADAM_WD, ADAM_STEP = 0.001, 0.9, 0.999, 1e-08, 0.01, 10
N_CHIPS, N_DEV = 4, 8

W_NAMES = ['c_ctx', 'w_ada', 'b_ada', 'w_in', 'b_in', 'q_gain', 'k_gain', 'conv_dw_w', 'conv_dw_b', 'conv_ln_g',
           'conv_ln_b', 'conv_pw_w', 'conv_pw_b', 'pool_w', 'pool_scale', 'w_out', 'b_out', 'ln1_g', 'ln1_b', 'ln2_g',
           'ln2_b', 'w_up', 'ffn_dw_w', 'ffn_dw_b', 'w_down']
BIG = {'w_in': 2, 'conv_pw_w': 1, 'pool_w': 2, 'w_out': 1, 'w_up': 2, 'w_down': 1}
SMALL_SHARDED = {'conv_dw_w': 2, 'ffn_dw_w': 2}
SMALL = ['b_in', 'q_gain', 'k_gain', 'conv_dw_w', 'conv_dw_b', 'conv_ln_g', 'conv_ln_b', 'conv_pw_b', 'pool_scale',
         'b_out', 'ln1_g', 'ln1_b', 'ln2_g', 'ln2_b', 'ffn_dw_w', 'ffn_dw_b']


def _tile(n, pref, align):
    if n <= pref:
        return n
    t = (pref // align) * align
    while t >= align:
        if n % t == 0:
            return t
        t -= align
    return n


def _call(body, *, name, out_shape, in_specs, out_specs, grid=None, scratch=()):
    kw = {}
    if grid is not None:
        kw['grid'] = grid
        kw['compiler_params'] = pltpu.CompilerParams(dimension_semantics=('arbitrary',) * len(grid),
                                                     vmem_limit_bytes=VMEM_LIMIT_BYTES)
    return pl.pallas_call(body, name=name, out_shape=out_shape, in_specs=in_specs, out_specs=out_specs,
                          scratch_shapes=list(scratch), **kw)


def _mm(a, b, mode, bias=None, out_dtype=F32, name='mm'):
    if mode == 'nn':
        (M, K), (K2, N) = a.shape, b.shape
    elif mode == 'nt':
        (M, K), (N, K2) = a.shape, b.shape
    else:
        (K, M), (K2, N) = a.shape, b.shape
    assert K == K2, (a.shape, b.shape, mode)
    tm = _tile(M, 1024, 128 if mode == 'tn' else 16)
    tn = _tile(N, 1408, 128)
    tk = _tile(K, 1024, 16 if mode == 'tn' else 128)
    nk = K // tk
    if mode == 'nn':
        a_spec = pl.BlockSpec((tm, tk), lambda i, j, k: (i, k))
        b_spec = pl.BlockSpec((tk, tn), lambda i, j, k: (k, j))
        dims = ((1,), (0,))
    elif mode == 'nt':
        a_spec = pl.BlockSpec((tm, tk), lambda i, j, k: (i, k))
        b_spec = pl.BlockSpec((tn, tk), lambda i, j, k: (j, k))
        dims = ((1,), (1,))
    else:
        a_spec = pl.BlockSpec((tk, tm), lambda i, j, k: (k, i))
        b_spec = pl.BlockSpec((tk, tn), lambda i, j, k: (k, j))
        dims = ((0,), (0,))
    has_bias = bias is not None

    def body(*refs):
        if has_bias:
            a_ref, b_ref, bias_ref, o_ref, acc = refs
        else:
            a_ref, b_ref, o_ref, acc = refs
        kk = pl.program_id(2)

        @pl.when(kk == 0)
        def _():
            acc[...] = jnp.zeros_like(acc)

        acc[...] += lax.dot_general(a_ref[...].astype(BF16), b_ref[...].astype(BF16), (dims, ((), ())),
                                    preferred_element_type=F32)

        @pl.when(kk == nk - 1)
        def _():
            r = acc[...]
            if has_bias:
                r = r + bias_ref[...]
            o_ref[...] = r.astype(out_dtype)

    in_specs = [a_spec, b_spec]
    args = [a, b]
    if has_bias:
        in_specs.append(pl.BlockSpec((1, tn), lambda i, j, k: (0, j)))
        args.append(bias.reshape(1, N).astype(F32))
    return _call(body, name=name, grid=(M // tm, N // tn, nk), in_specs=in_specs,
                 out_specs=pl.BlockSpec((tm, tn), lambda i, j, k: (i, j)),
                 out_shape=jax.ShapeDtypeStruct((M, N), out_dtype),
                 scratch=[pltpu.VMEM((tm, tn), F32)])(*args)


def _colsum(x, name='colsum'):
    M, N = x.shape
    tm, tn = _tile(M, 1024, 8), _tile(N, 1024, 128)

    def body(x_ref, o_ref):
        @pl.when(pl.program_id(1) == 0)
        def _():
            o_ref[...] = jnp.zeros_like(o_ref)

        o_ref[...] += jnp.sum(x_ref[...].astype(F32), axis=0, keepdims=True)

    return _call(body, name=name, grid=(N // tn, M // tm),
                 in_specs=[pl.BlockSpec((tm, tn), lambda j, i: (i, j))],
                 out_specs=pl.BlockSpec((1, tn), lambda j, i: (0, j)),
                 out_shape=jax.ShapeDtypeStruct((1, N), F32))(x)


@jax.custom_vjp
def linear(x, w, b):
    return _mm(x, w, 'nn', bias=b, name='lin_fwd')


def _linear_fwd(x, w, b):
    return _mm(x, w, 'nn', bias=b, name='lin_fwd'), (x, w)


def _linear_bwd(res, dy):
    x, w = res
    return (_mm(dy, w, 'nt', name='lin_dx'), _mm(x, dy, 'tn', out_dtype=w.dtype, name='lin_dw'),
            _colsum(dy, name='lin_db')[0])


linear.defvjp(_linear_fwd, _linear_bwd)


@jax.custom_vjp
def linear_nb(x, w):
    return _mm(x, w, 'nn', name='linnb_fwd')


def _linear_nb_fwd(x, w):
    return _mm(x, w, 'nn', name='linnb_fwd'), (x, w)


def _linear_nb_bwd(res, dy):
    x, w = res
    return _mm(dy, w, 'nt', name='linnb_dx'), _mm(x, dy, 'tn', out_dtype=w.dtype, name='linnb_dw')


linear_nb.defvjp(_linear_nb_fwd, _linear_nb_bwd)


class Geom:
    def __init__(self, B, L, Lc, D, depth):
        self.B, self.L, self.Lc, self.D, self.depth = B, L, Lc, D, depth
        self.TL = _tile(math.gcd(L, Lc), 256, 8)
        self.tpe, self.tpc = L // self.TL, Lc // self.TL
        self.R = B * (L + Lc)
        self.NT = self.R // self.TL
        self.n_lat_tiles = B * self.tpe
        self.alpha = (2 * depth) ** 0.25

    def pidx(self, i):
        return jnp.minimum(i // self.tpe, self.B)

    def first_of_group(self, i):
        return jnp.where(i < self.n_lat_tiles, (i % self.tpe) == 0, i == self.n_lat_tiles)

    def seqpos(self, i):
        lat = i < self.n_lat_tiles
        pos0 = jnp.where(lat, (i % self.tpe) * self.TL, ((i - self.n_lat_tiles) % self.tpc) * self.TL)
        return pos0, jnp.where(lat, self.L, self.Lc)


def _make_rowop(g, fn, nr, nc, npar, name):
    def specs(rows, consts, params):
        hm = rows[0].shape[0] // g.R
        rs = [pl.BlockSpec((g.TL * hm, a.shape[1]), lambda i: (i, 0)) for a in list(rows) + list(consts)]
        ps = [pl.BlockSpec((None,) + a.shape[1:], lambda i: (g.pidx(i), 0, 0)) for a in params]
        return hm, rs, ps

    def out_struct(rows, consts, params, hm):
        tiles = [jax.ShapeDtypeStruct((g.TL * hm, a.shape[1]), F32) for a in list(rows) + list(consts)]
        tiles += [jax.ShapeDtypeStruct(a.shape[1:], F32) for a in params]
        return jax.eval_shape(fn, *tiles)

    def run_fwd(rows, consts, params):
        hm, rs, ps = specs(rows, consts, params)
        outs = out_struct(rows, consts, params, hm)
        n_in = nr + nc + npar

        def body(*refs):
            vals = [r[...] for r in refs[:n_in]]
            res = fn(*vals)
            for o_ref, v in zip(refs[n_in:], res):
                o_ref[...] = v

        return _call(body, name=name + '_fwd', grid=(g.NT,), in_specs=rs + ps,
                     out_specs=[pl.BlockSpec(o.shape, lambda i: (i, 0)) for o in outs],
                     out_shape=[jax.ShapeDtypeStruct((g.R * hm, o.shape[1]), F32) for o in outs])(
                         *rows, *consts, *params)

    def run_bwd(rows, consts, params, douts):
        hm, rs, ps = specs(rows, consts, params)
        outs = out_struct(rows, consts, params, hm)
        n_in = nr + nc + npar
        n_do = len(outs)

        def body(*refs):
            rv = [r[...] for r in refs[:nr]]
            cv = [r[...] for r in refs[nr:nr + nc]]
            pv = [r[...] for r in refs[nr + nc:n_in]]
            dov = tuple(r[...] for r in refs[n_in:n_in + n_do])
            drow_refs = refs[n_in + n_do:n_in + n_do + nr]
            dpar_refs = refs[n_in + n_do + nr:]

            def f(*d):
                return fn(*d[:nr], *cv, *d[nr:])

            _, vjp = jax.vjp(f, *rv, *pv)
            gr = vjp(dov)
            for r, v in zip(drow_refs, gr[:nr]):
                r[...] = v
            if npar:
                @pl.when(g.first_of_group(pl.program_id(0)))
                def _():
                    for r in dpar_refs:
                        r[...] = jnp.zeros_like(r)

                for r, v in zip(dpar_refs, gr[nr:]):
                    r[...] += v

        do_specs = [pl.BlockSpec(o.shape, lambda i: (i, 0)) for o in outs]
        res = _call(body, name=name + '_bwd', grid=(g.NT,), in_specs=rs + ps + do_specs,
                    out_specs=rs[:nr] + ps,
                    out_shape=[jax.ShapeDtypeStruct(a.shape, F32) for a in list(rows) + list(params)])(
                        *rows, *consts, *params, *douts)
        return tuple(res[:nr]), tuple(res[nr:])

    @jax.custom_vjp
    def op(rows, consts, params):
        return tuple(run_fwd(rows, consts, params))

    def op_fwd(rows, consts, params):
        return tuple(run_fwd(rows, consts, params)), (rows, consts, params)

    def op_bwd(res, douts):
        rows, consts, params = res
        drows, dpars = run_bwd(rows, consts, params, douts)
        return drows, tuple(jnp.zeros_like(c) for c in consts), dpars

    op.defvjp(op_fwd, op_bwd)
    return op


def _lane_iota(shape):
    return lax.broadcasted_iota(jnp.int32, shape, len(shape) - 1)


@jax.custom_vjp
def _swap_halves(t):
    w = t.shape[-1]
    first = (_lane_iota(t.shape) % 64) < 32
    return jnp.where(first, pltpu.roll(t, w - 32, t.ndim - 1), pltpu.roll(t, 32, t.ndim - 1))


_swap_halves.defvjp(lambda t: (_swap_halves(t), None), lambda _, ct: (_swap_halves(ct),))


def _ln(t, gain, bias):
    mu = jnp.mean(t, axis=-1, keepdims=True)
    d = t - mu
    var = jnp.mean(d * d, axis=-1, keepdims=True)
    return d * lax.rsqrt(var + LN_EPS) * gain + bias


def _sig(t):
    return jax.nn.sigmoid(t)


def _build_rowops(g):
    alpha = g.alpha

    def f_mod(x, shift, scale):
        return (x * (1.0 + scale) + shift,)

    def f_qk(z, cos, sin, gain):
        ms = jnp.mean(z * z, axis=-1, keepdims=True)
        t = z * lax.rsqrt(ms + RMS_EPS) * gain
        return (t * cos + _swap_halves(t) * sin,)

    def f_glu(a, gt):
        return (a * _sig(gt),)

    def f_lnsilu(c1, gain, bias):
        y = _ln(c1, gain, bias)
        return (y * _sig(y),)

    def f_merge(g0, g1, g2, at, co, mx, ps):
        return (_sig(g0) * at + _sig(g1) * co + _sig(g2) * (mx * ps),)

    def f_resln(x, y, gate, gain, bias):
        return (_ln(alpha * x + gate * y, gain, bias),)

    def f_silumul(a, u):
        return (a * _sig(a) * u,)

    mk = functools.partial(_make_rowop, g)
    return dict(mod=mk(f_mod, 1, 0, 2, 'mod'), qk=mk(f_qk, 1, 2, 1, 'qk'), glu=mk(f_glu, 2, 0, 0, 'glu'),
                lnsilu=mk(f_lnsilu, 1, 0, 2, 'lnsilu'), merge=mk(f_merge, 6, 0, 1, 'merge'),
                resln=mk(f_resln, 2, 0, 3, 'resln'), silumul=mk(f_silumul, 2, 0, 0, 'silumul'))


HALO = 16
LANES = 128


def _halo_specs(g, tc, rows_first=True):
    r = g.TL // HALO
    last = g.R // HALO - 1

    def mk(shape, fn):
        return pl.BlockSpec(shape, fn if rows_first else (lambda j, i: fn(i, j)))

    return [mk((HALO, tc), lambda i, j: (jnp.maximum(i * r - 1, 0), j)),
            mk((g.TL, tc), lambda i, j: (i, j)),
            mk((HALO, tc), lambda i, j: (jnp.minimum((i + 1) * r, last), j))]


def _extended(g, i, top, mid, bot):
    pos0, ls = g.seqpos(i)
    top = jnp.where(pos0 > 0, top, 0.0)
    bot = jnp.where(pos0 + g.TL < ls, bot, 0.0)
    return jnp.concatenate([top, mid, bot], axis=0)


def _tap(ext, off, TL):
    if off == 0:
        return ext[HALO:HALO + TL]
    return pltpu.roll(ext, (-off) % ext.shape[0], 0)[HALO:HALO + TL]


def _conv_call(g, x, w, b, name):
    K, C = w.shape
    p = (K - 1) // 2
    tc = _tile(C, 512, LANES)
    TL = g.TL
    assert p <= HALO and TL % HALO == 0 and tc % LANES == 0

    def body(xp_ref, xc_ref, xn_ref, w_ref, b_ref, o_ref):
        i = pl.program_id(0)
        for c0 in range(0, tc, LANES):
            sl = slice(c0, c0 + LANES)
            ext = _extended(g, i, xp_ref[:, sl], xc_ref[:, sl], xn_ref[:, sl])
            acc = jnp.zeros((TL, LANES), F32) + b_ref[:, sl]
            for k in range(K):
                acc = acc + _tap(ext, k - p, TL) * w_ref[pl.ds(k, 1), sl]
            o_ref[:, sl] = acc

    return _call(body, name=name, grid=(g.NT, C // tc),
                 in_specs=_halo_specs(g, tc) + [pl.BlockSpec((K, tc), lambda i, j: (0, j)),
                                                pl.BlockSpec((1, tc), lambda i, j: (0, j))],
                 out_specs=pl.BlockSpec((TL, tc), lambda i, j: (i, j)),
                 out_shape=jax.ShapeDtypeStruct(x.shape, F32))(x, x, x, w, b.reshape(1, C))


def _conv_dw_call(g, x, dy, K, name):
    C = x.shape[1]
    p = (K - 1) // 2
    tc = _tile(C, 512, LANES)
    TL = g.TL
    assert p <= HALO and TL % HALO == 0 and tc % LANES == 0

    def body(xp_ref, xc_ref, xn_ref, dy_ref, dw_ref, db_ref):
        i = pl.program_id(1)

        @pl.when(i == 0)
        def _():
            dw_ref[...] = jnp.zeros_like(dw_ref)
            db_ref[...] = jnp.zeros_like(db_ref)

        for c0 in range(0, tc, LANES):
            sl = slice(c0, c0 + LANES)
            ext = _extended(g, i, xp_ref[:, sl], xc_ref[:, sl], xn_ref[:, sl])
            dy_v = dy_ref[:, sl]
            db_ref[:, sl] += jnp.sum(dy_v, axis=0, keepdims=True)
            for k in range(K):
                dw_ref[pl.ds(k, 1), sl] += jnp.sum(_tap(ext, k - p, TL) * dy_v, axis=0, keepdims=True)

    return _call(body, name=name, grid=(C // tc, g.NT),
                 in_specs=_halo_specs(g, tc, rows_first=False) + [pl.BlockSpec((TL, tc), lambda j, i: (i, j))],
                 out_specs=[pl.BlockSpec((K, tc), lambda j, i: (0, j)), pl.BlockSpec((1, tc), lambda j, i: (0, j))],
                 out_shape=[jax.ShapeDtypeStruct((K, C), F32), jax.ShapeDtypeStruct((1, C), F32)])(x, x, x, dy)


def _pool_call(g, u, transpose, name):
    C = u.shape[1]
    cg = C // len(POOL_WINDOWS)
    TL = g.TL
    assert max(POOL_WINDOWS) // 2 <= HALO and TL % HALO == 0 and cg % LANES == 0

    def body(up_ref, uc_ref, un_ref, o_ref):
        i = pl.program_id(0)
        pos0, ls = g.seqpos(i)
        grp = pl.program_id(1)
        pos_e = pos0 - HALO + lax.broadcasted_iota(jnp.int32, (TL + 2 * HALO, LANES), 0)
        for gi, wdw in enumerate(POOL_WINDOWS):
            @pl.when(grp == gi)
            def _(wdw=wdw):
                half = wdw // 2
                cnt = jnp.minimum(pos_e - half + wdw, ls) - jnp.maximum(pos_e - half, 0)
                cnt = jnp.maximum(cnt, 1).astype(F32)
                for c0 in range(0, cg, LANES):
                    sl = slice(c0, c0 + LANES)
                    ext = _extended(g, i, up_ref[:, sl], uc_ref[:, sl], un_ref[:, sl])
                    if transpose:
                        ext = ext / cnt
                        offs = range(-half + 1, half + 1)
                    else:
                        offs = range(-half, half)
                    acc = jnp.zeros((TL, LANES), F32)
                    for off in offs:
                        acc = acc + _tap(ext, off, TL)
                    if transpose:
                        o_ref[:, sl] = acc - uc_ref[:, sl]
                    else:
                        o_ref[:, sl] = acc / cnt[HALO:HALO + TL] - uc_ref[:, sl]

    return _call(body, name=name, grid=(g.NT, len(POOL_WINDOWS)), in_specs=_halo_specs(g, cg),
                 out_specs=pl.BlockSpec((TL, cg), lambda i, j: (i, j)),
                 out_shape=jax.ShapeDtypeStruct(u.shape, F32))(u, u, u)


def _build_seqops(g):
    @jax.custom_vjp
    def dwconv(x, w, b):
        return _conv_call(g, x, w, b, 'conv_fwd')

    def dwconv_fwd(x, w, b):
        return _conv_call(g, x, w, b, 'conv_fwd'), (x, w)

    def dwconv_bwd(res, dy):
        x, w = res
        dx = _conv_call(g, dy, w[::-1], jnp.zeros((w.shape[1],), F32), 'conv_dx')
        dw, db = _conv_dw_call(g, x, dy, w.shape[0], 'conv_dw')
        return dx, dw, db[0]

    dwconv.defvjp(dwconv_fwd, dwconv_bwd)

    @jax.custom_vjp
    def pool(u):
        return _pool_call(g, u, False, 'pool_fwd')

    pool.defvjp(lambda u: (_pool_call(g, u, False, 'pool_fwd'), None),
                lambda _, dp: (_pool_call(g, dp, True, 'pool_bwd'),))
    return dwconv, pool


def _attn_probs(qg, ks, scale):
    ss = [lax.dot_general(qg, k, (((1,), (1,)), ((), ())), preferred_element_type=F32) * scale for k in ks]
    m = ss[0].max(axis=-1, keepdims=True)
    for s in ss[1:]:
        m = jnp.maximum(m, s.max(axis=-1, keepdims=True))
    es = [jnp.exp(s - m) for s in ss]
    den = es[0].sum(axis=-1, keepdims=True)
    for e in es[1:]:
        den = den + e.sum(axis=-1, keepdims=True)
    inv = 1.0 / den
    return [e * inv for e in es]


def _attn_specs(B, q, kvs):
    lq = q.shape[0] // B
    kvh = kvs[0].shape[1] // HEAD_DIM
    grp = q.shape[1] // HEAD_DIM // kvh
    tq = _tile(lq, 256, 8)
    nq = lq // tq
    q_spec = pl.BlockSpec((tq, grp * HEAD_DIM), lambda b, h, i: (b * nq + i, h))
    kv_specs = [pl.BlockSpec((a.shape[0] // B, HEAD_DIM), lambda b, h, i: (b, h)) for a in kvs]
    return kvh, grp, tq, nq, q_spec, kv_specs


def _attn_fwd_call(B, q, kvs, name):
    kvh, grp, tq, nq, q_spec, kv_specs = _attn_specs(B, q, kvs)
    ns = len(kvs) // 2
    scale = HEAD_DIM ** -0.5

    def body(*refs):
        q_ref, o_ref = refs[0], refs[-1]
        ks = [refs[1 + 2 * s][...].astype(BF16) for s in range(ns)]
        vs = [refs[2 + 2 * s][...].astype(BF16) for s in range(ns)]
        for gi in range(grp):
            sl = slice(gi * HEAD_DIM, (gi + 1) * HEAD_DIM)
            ps = _attn_probs(q_ref[:, sl].astype(BF16), ks, scale)
            o = jnp.zeros((tq, HEAD_DIM), F32)
            for p, v in zip(ps, vs):
                o = o + jnp.dot(p.astype(BF16), v, preferred_element_type=F32)
            o_ref[:, sl] = o

    return _call(body, name=name, grid=(B, kvh, nq), in_specs=[q_spec] + kv_specs, out_specs=q_spec,
                 out_shape=jax.ShapeDtypeStruct(q.shape, F32))(q, *kvs)


def _attn_bwd_call(B, q, kvs, do, name):
    kvh, grp, tq, nq, q_spec, kv_specs = _attn_specs(B, q, kvs)
    ns = len(kvs) // 2
    scale = HEAD_DIM ** -0.5

    def body(*refs):
        q_ref, do_ref = refs[0], refs[1 + 2 * ns]
        dq_ref = refs[2 + 2 * ns]
        dkv_refs = refs[3 + 2 * ns:]
        ks = [refs[1 + 2 * s][...].astype(BF16) for s in range(ns)]
        vs = [refs[2 + 2 * s][...].astype(BF16) for s in range(ns)]

        @pl.when(pl.program_id(2) == 0)
        def _():
            for r in dkv_refs:
                r[...] = jnp.zeros_like(r)

        for gi in range(grp):
            sl = slice(gi * HEAD_DIM, (gi + 1) * HEAD_DIM)
            qg = q_ref[:, sl].astype(BF16)
            dog = do_ref[:, sl].astype(BF16)
            ps = _attn_probs(qg, ks, scale)
            dps = [lax.dot_general(dog, v, (((1,), (1,)), ((), ())), preferred_element_type=F32) for v in vs]
            delta = (ps[0] * dps[0]).sum(axis=-1, keepdims=True)
            for p, dp in zip(ps[1:], dps[1:]):
                delta = delta + (p * dp).sum(axis=-1, keepdims=True)
            dq = jnp.zeros((tq, HEAD_DIM), F32)
            for s in range(ns):
                ds = (ps[s] * (dps[s] - delta) * scale).astype(BF16)
                dq = dq + jnp.dot(ds, ks[s], preferred_element_type=F32)
                dkv_refs[2 * s][...] += lax.dot_general(ds, qg, (((0,), (0,)), ((), ())),
                                                        preferred_element_type=F32)
                dkv_refs[2 * s + 1][...] += lax.dot_general(ps[s].astype(BF16), dog, (((0,), (0,)), ((), ())),
                                                            preferred_element_type=F32)
            dq_ref[:, sl] = dq

    res = _call(body, name=name, grid=(B, kvh, nq), in_specs=[q_spec] + kv_specs + [q_spec],
                out_specs=[q_spec] + kv_specs,
                out_shape=[jax.ShapeDtypeStruct(a.shape, F32) for a in [q] + list(kvs)])(q, *kvs, do)
    return tuple(res)


def _build_attn(B):
    @jax.custom_vjp
    def attn(q, kvs):
        return _attn_fwd_call(B, q, kvs, 'attn%d_fwd' % (len(kvs) // 2))

    def attn_fwd(q, kvs):
        return attn(q, kvs), (q, kvs)

    def attn_bwd(res, do):
        q, kvs = res
        out = _attn_bwd_call(B, q, kvs, do, 'attn%d_bwd' % (len(kvs) // 2))
        return out[0], tuple(out[1:])

    attn.defvjp(attn_fwd, attn_bwd)
    return attn


def _ew(fn, arrays, out_dtypes, name):
    rows, cols = arrays[0].shape
    tr = _tile(rows, 256, 16)
    n_in = len(arrays)

    def body(*refs):
        res = fn(*[r[...] for r in refs[:n_in]])
        for r, v in zip(refs[n_in:], res):
            r[...] = v.astype(r.dtype)

    spec = pl.BlockSpec((tr, cols), lambda i: (i, 0))
    return _call(body, name=name, grid=(rows // tr,), in_specs=[spec] * n_in, out_specs=[spec] * len(out_dtypes),
                 out_shape=[jax.ShapeDtypeStruct((rows, cols), d) for d in out_dtypes])(*arrays)


def _as2d(a):
    return a.reshape(-1, a.shape[-1]) if a.ndim > 1 else a.reshape(1, -1)


def _sum_slots(x, name):
    n, rows, cols = x.shape
    tr = _tile(rows, 256, 16)

    def body(x_ref, o_ref):
        acc = x_ref[0].astype(F32)
        for s in range(1, n):
            acc = acc + x_ref[s].astype(F32)
        o_ref[...] = acc

    return _call(body, name=name, grid=(rows // tr,), in_specs=[pl.BlockSpec((n, tr, cols), lambda i: (0, i, 0))],
                 out_specs=pl.BlockSpec((tr, cols), lambda i: (i, 0)),
                 out_shape=jax.ShapeDtypeStruct((rows, cols), F32))(x)


def _adamw(w, gr, m, v, name):
    b1c = 1.0 - ADAM_B1 ** ADAM_STEP
    b2c = 1.0 - ADAM_B2 ** ADAM_STEP

    def fn(w_, g_, m_, v_):
        mn = ADAM_B1 * m_ + (1.0 - ADAM_B1) * g_
        vn = ADAM_B2 * v_ + (1.0 - ADAM_B2) * (g_ * g_)
        delta = -ADAM_LR * ((mn / b1c) / (jnp.sqrt(vn / b2c) + ADAM_EPS) + ADAM_WD * w_)
        return delta, mn, vn

    shp = w.shape
    outs = _ew(fn, [_as2d(a) for a in (w, gr, m, v)], [F32, F32, F32], name)
    return [o.reshape(shp) for o in outs]


def _loss_call(y, target):
    rows, D = y.shape
    tr = _tile(rows, 512, 8)

    def body(y_ref, t_ref, dy_ref, l_ref):
        @pl.when(pl.program_id(0) == 0)
        def _():
            l_ref[...] = jnp.zeros_like(l_ref)

        e = y_ref[...] - t_ref[...]
        dy_ref[...] = e * (1.0 / D)
        s = jnp.sum(jnp.sum(e * e, axis=1, keepdims=True), axis=0, keepdims=True)
        l_ref[...] += s * (0.5 / D)

    spec = pl.BlockSpec((tr, D), lambda i: (i, 0))
    dy, lsum = _call(body, name='loss', grid=(rows // tr,), in_specs=[spec, spec],
                     out_specs=[spec, pl.BlockSpec((1, 128), lambda i: (0, 0))],
                     out_shape=[jax.ShapeDtypeStruct((rows, D), F32), jax.ShapeDtypeStruct((1, 128), F32)])(y, target)
    return dy, lsum[0, 0]


ANY = pl.BlockSpec(memory_space=pl.ANY)


def _coords():
    return lax.axis_index('x'), lax.axis_index('y'), lax.axis_index('c')


def _gather_chips(arrs, name):
    n = len(arrs)
    rel = [(1, 0), (0, 1), (1, 1)]

    def body(*refs):
        ins, outs = refs[:n], refs[n:2 * n]
        send, recv, loc = refs[2 * n:]
        x, y, c = _coords()
        me = 2 * x + y
        local = [pltpu.make_async_copy(ins[a], outs[a].at[me], loc.at[a]) for a in range(n)]
        for cp in local:
            cp.start()
        remote = []
        for a in range(n):
            for k, (dx, dy) in enumerate(rel):
                cp = pltpu.make_async_remote_copy(src_ref=ins[a], dst_ref=outs[a].at[me], send_sem=send.at[a * 3 + k],
                                                  recv_sem=recv.at[a * 3 + k],
                                                  device_id=((x + dx) % 2, (y + dy) % 2, c), device_id_type=MESH)
                cp.start()
                remote.append(cp)
        for cp in remote:
            cp.wait()
        for cp in local:
            cp.wait()

    return pl.pallas_call(body, name=name, in_specs=[ANY] * n, out_specs=[ANY] * n,
                          out_shape=[jax.ShapeDtypeStruct((N_CHIPS,) + a.shape, a.dtype) for a in arrs],
                          scratch_shapes=[pltpu.SemaphoreType.DMA((3 * n,)), pltpu.SemaphoreType.DMA((3 * n,)),
                                          pltpu.SemaphoreType.DMA((n,))])(*arrs)


def _gather_chips_halved(arrs, name):
    n = len(arrs)
    rel = [(1, 0), (0, 1), (1, 1)]
    for a in arrs:
        assert a.ndim >= 3 and a.shape[0] % 2 == 0, a.shape

    def body(*refs):
        ins, outs = refs[:n], refs[n:2 * n]
        send, recv, fsend, frecv, loc = refs[2 * n:]
        x, y, c = _coords()
        me = 2 * x + y
        local = [pltpu.make_async_copy(ins[a], outs[a].at[me], loc.at[a]) for a in range(n)]
        for cp in local:
            cp.start()
        first, passed = [], []
        for a in range(n):
            hn = arrs[a].shape[0] // 2
            mine, theirs = pl.ds(c * hn, hn), pl.ds((1 - c) * hn, hn)
            for k, (dx, dy) in enumerate(rel):
                tx, ty = (x + dx) % 2, (y + dy) % 2
                src = 2 * tx + ty
                s = a * 3 + k
                first.append(pltpu.make_async_remote_copy(
                    src_ref=ins[a].at[mine], dst_ref=outs[a].at[me, mine], send_sem=send.at[s], recv_sem=recv.at[s],
                    device_id=(tx, ty, c), device_id_type=MESH))
                passed.append((
                    pltpu.make_async_remote_copy(
                        src_ref=outs[a].at[src, mine], dst_ref=outs[a].at[src, mine], send_sem=fsend.at[s],
                        recv_sem=frecv.at[s], device_id=(x, y, 1 - c), device_id_type=MESH),
                    pltpu.make_async_remote_copy(
                        src_ref=outs[a].at[src, theirs], dst_ref=outs[a].at[src, theirs], send_sem=fsend.at[s],
                        recv_sem=frecv.at[s], device_id=(x, y, 1 - c), device_id_type=MESH),
                    pltpu.make_async_remote_copy(
                        src_ref=ins[a].at[mine], dst_ref=outs[a].at[src, mine], send_sem=send.at[s],
                        recv_sem=recv.at[s], device_id=(tx, ty, c), device_id_type=MESH)))
        for cp in first:
            cp.start()
        for fwd, _, landed in passed:
            landed.wait_recv()
            fwd.start()
        for cp in first:
            cp.wait_send()
        for fwd, from_sibling, _ in passed:
            fwd.wait_send()
            from_sibling.wait_recv()
        for cp in local:
            cp.wait()

    return pl.pallas_call(body, name=name, in_specs=[ANY] * n, out_specs=[ANY] * n,
                          out_shape=[jax.ShapeDtypeStruct((N_CHIPS,) + a.shape, a.dtype) for a in arrs],
                          scratch_shapes=[pltpu.SemaphoreType.DMA((3 * n,)), pltpu.SemaphoreType.DMA((3 * n,)),
                                          pltpu.SemaphoreType.DMA((3 * n,)), pltpu.SemaphoreType.DMA((3 * n,)),
                                          pltpu.SemaphoreType.DMA((n,))])(*arrs)


def _gather_all(arr, name):
    def body(in_ref, out_ref, send, recv, loc):
        x, y, c = _coords()
        me = 4 * x + 2 * y + c
        local = pltpu.make_async_copy(in_ref, out_ref.at[me], loc.at[0])
        local.start()
        remote = []
        for k in range(1, N_DEV):
            dx, dy, dc = (k >> 2) & 1, (k >> 1) & 1, k & 1
            cp = pltpu.make_async_remote_copy(src_ref=in_ref, dst_ref=out_ref.at[me], send_sem=send.at[k - 1],
                                              recv_sem=recv.at[k - 1],
                                              device_id=((x + dx) % 2, (y + dy) % 2, (c + dc) % 2), device_id_type=MESH)
            cp.start()
            remote.append(cp)
        for cp in remote:
            cp.wait()
        local.wait()

    return pl.pallas_call(body, name=name, in_specs=[ANY], out_specs=ANY,
                          out_shape=jax.ShapeDtypeStruct((N_DEV,) + arr.shape, arr.dtype),
                          scratch_shapes=[pltpu.SemaphoreType.DMA((N_DEV - 1,)), pltpu.SemaphoreType.DMA((N_DEV - 1,)),
                                          pltpu.SemaphoreType.DMA((1,))])(arr)


def _exchange_shards(arrs, name):
    n = len(arrs)
    rel = [(1, 0), (0, 1), (1, 1)]

    def body(*refs):
        ins, outs = refs[:n], refs[n:2 * n]
        send, recv, loc, sib_send, sib_recv = refs[2 * n:]
        x, y, c = _coords()
        me = 2 * x + y
        local = [pltpu.make_async_copy(ins[a].at[me], outs[a].at[c, me], loc.at[a]) for a in range(n)]
        for cp in local:
            cp.start()
        remote = []
        for a in range(n):
            for k, (dx, dy) in enumerate(rel):
                tx, ty = (x + dx) % 2, (y + dy) % 2
                cp = pltpu.make_async_remote_copy(src_ref=ins[a].at[2 * tx + ty], dst_ref=outs[a].at[c, me],
                                                  send_sem=send.at[a * 3 + k], recv_sem=recv.at[a * 3 + k],
                                                  device_id=(tx, ty, c), device_id_type=MESH)
                cp.start()
                remote.append(cp)
        for cp in remote:
            cp.wait()
        for cp in local:
            cp.wait()
        sib = [pltpu.make_async_remote_copy(src_ref=outs[a].at[c], dst_ref=outs[a].at[c], send_sem=sib_send.at[a],
                                            recv_sem=sib_recv.at[a], device_id=(x, y, 1 - c), device_id_type=MESH)
               for a in range(n)]
        for cp in sib:
            cp.start()
        for a in range(n):
            sib[a].wait_send()
            pltpu.make_async_remote_copy(src_ref=outs[a].at[1 - c], dst_ref=outs[a].at[1 - c], send_sem=sib_send.at[a],
                                         recv_sem=sib_recv.at[a], device_id=(x, y, 1 - c),
                                         device_id_type=MESH).wait_recv()

    return pl.pallas_call(body, name=name, in_specs=[ANY] * n, out_specs=[ANY] * n,
                          out_shape=[jax.ShapeDtypeStruct((2,) + a.shape, a.dtype) for a in arrs],
                          scratch_shapes=[pltpu.SemaphoreType.DMA((3 * n,)), pltpu.SemaphoreType.DMA((3 * n,)),
                                          pltpu.SemaphoreType.DMA((n,)), pltpu.SemaphoreType.DMA((n,)),
                                          pltpu.SemaphoreType.DMA((n,))])(*arrs)


def _rope_tables(g, heads):
    t = jnp.arange(g.L, dtype=F32)
    row, col = jnp.floor(t / GRID_W), t % GRID_W
    inv = ROPE_THETA ** (-jnp.arange(ROPE_PAIRS, dtype=F32) / ROPE_PAIRS)
    ang = jnp.stack([row, col], axis=-1)[..., None] * inv
    cos = jnp.broadcast_to(jnp.cos(ang)[:, :, None, :], (g.L, 2, 2, ROPE_PAIRS)).reshape(g.L, HEAD_DIM)
    sin = jnp.sin(ang)[:, :, None, :] * jnp.array([-1.0, 1.0], F32)[None, None, :, None]
    sin = sin.reshape(g.L, HEAD_DIM)
    n_ctx = g.B * g.Lc
    cos = jnp.concatenate([jnp.tile(cos, (g.B, 1)), jnp.ones((n_ctx, HEAD_DIM), F32)])
    sin = jnp.concatenate([jnp.tile(sin, (g.B, 1)), jnp.zeros((n_ctx, HEAD_DIM), F32)])
    return jnp.repeat(cos, heads, axis=0), jnp.repeat(sin, heads, axis=0)


def _forward(g, ops, big, small, mods, x, ctx):
    D, B = g.D, g.B
    rop, dwconv, pool, attn = ops
    nh = D // HEAD_DIM
    nkv = nh // 4
    kvw = nkv * HEAD_DIM
    q_end, k_end, v_end = D, D + kvw, D + 2 * kvw
    conv_end = v_end + 2 * D
    pool_end = conv_end + D
    cg = D // len(POOL_WINDOWS)
    n_lat = B * g.L
    cos_q, sin_q = _rope_tables(g, nh)
    cos_k, sin_k = _rope_tables(g, nkv)
    xa = jnp.concatenate([x.reshape(n_lat, D), ctx.reshape(B * g.Lc, D)])

    def shared(v):
        return jnp.broadcast_to(v[None, None, :], (B + 1, 1, v.shape[0]))

    for l in range(g.depth):
        md = [mods[l, :, j][:, None, :] for j in range(6)]
        w_in, b_in = big['w_in'][l], small['b_in'][l]

        def proj(h, a, b):
            return linear(h, w_in[:, a:b], b_in[a:b])

        (h,) = rop['mod']((xa,), (), (md[0], md[1]))
        zq, zk, zv = proj(h, 0, q_end), proj(h, q_end, k_end), proj(h, k_end, v_end)
        (qh,) = rop['qk']((zq.reshape(-1, HEAD_DIM),), (cos_q, sin_q), (shared(small['q_gain'][l]),))
        (kh,) = rop['qk']((zk.reshape(-1, HEAD_DIM),), (cos_k, sin_k), (shared(small['k_gain'][l]),))
        qh, kh = qh.reshape(-1, D), kh.reshape(-1, kvw)
        o_lat = attn(qh[:n_lat], (kh[n_lat:], zv[n_lat:], kh[:n_lat], zv[:n_lat]))
        o_ctx = attn(qh[n_lat:], (kh[n_lat:], zv[n_lat:]))
        attn_o = jnp.concatenate([o_lat, o_ctx])

        (glu,) = rop['glu']((proj(h, v_end, v_end + D), proj(h, v_end + D, conv_end)), (), ())
        c1 = dwconv(glu, small['conv_dw_w'][l], small['conv_dw_b'][l])
        (c2,) = rop['lnsilu']((c1,), (), (shared(small['conv_ln_g'][l]), shared(small['conv_ln_b'][l])))
        conv_o = linear(c2, big['conv_pw_w'][l], small['conv_pw_b'][l])

        pooled = pool(proj(h, conv_end, pool_end))
        w_bd = jnp.zeros((D, D), BF16)
        for gi in range(len(POOL_WINDOWS)):
            w_bd = lax.dynamic_update_slice(w_bd, big['pool_w'][l, gi], (gi * cg, gi * cg))
        mixed = linear_nb(pooled, w_bd)

        gates = [proj(h, pool_end + j * D, pool_end + (j + 1) * D) for j in range(3)]
        (m,) = rop['merge']((gates[0], gates[1], gates[2], attn_o, conv_o, mixed), (),
                            (shared(small['pool_scale'][l]),))
        y = linear(m, big['w_out'][l], small['b_out'][l])
        (x1,) = rop['resln']((xa, y), (), (md[2], shared(small['ln1_g'][l]), shared(small['ln1_b'][l])))

        (h2,) = rop['mod']((x1,), (), (md[3], md[4]))
        dff = big['w_up'].shape[2] // 2
        ua = linear_nb(h2, big['w_up'][l][:, :dff])
        uu = linear_nb(h2, big['w_up'][l][:, dff:])
        a = dwconv(ua, small['ffn_dw_w'][l], small['ffn_dw_b'][l])
        (f,) = rop['silumul']((a, uu), (), ())
        y2 = linear_nb(f, big['w_down'][l])
        (xa,) = rop['resln']((x1, y2), (), (md[5], shared(small['ln2_g'][l]), shared(small['ln2_b'][l])))
    return xa[:n_lat]


def _silu(z):
    return (z * _sig(z),)


def _dsilu(z, d):
    s = _sig(z)
    return (d * s * (1.0 + z * (1.0 - s)),)


def kernel(x, c, ctx, c_ctx, w_ada, b_ada, w_in, b_in, q_gain, k_gain, conv_dw_w, conv_dw_b, conv_ln_g, conv_ln_b, conv_pw_w, conv_pw_b, pool_w, pool_scale, w_out, b_out, ln1_g, ln1_b, ln2_g, ln2_b, w_up, ffn_dw_w, ffn_dw_b, w_down, loss_target, m_c_ctx, m_w_ada, m_b_ada, m_w_in, m_b_in, m_q_gain, m_k_gain, m_conv_dw_w, m_conv_dw_b, m_conv_ln_g, m_conv_ln_b, m_conv_pw_w, m_conv_pw_b, m_pool_w, m_pool_scale, m_w_out, m_b_out, m_ln1_g, m_ln1_b, m_ln2_g, m_ln2_b, m_w_up, m_ffn_dw_w, m_ffn_dw_b, m_w_down, v_c_ctx, v_w_ada, v_b_ada, v_w_in, v_b_in, v_q_gain, v_k_gain, v_conv_dw_w, v_conv_dw_b, v_conv_ln_g, v_conv_ln_b, v_conv_pw_w, v_conv_pw_b, v_pool_w, v_pool_scale, v_w_out, v_b_out, v_ln1_g, v_ln1_b, v_ln2_g, v_ln2_b, v_w_up, v_ffn_dw_w, v_ffn_dw_b, v_w_down):
    A = dict(locals())
    B, L, D = x.shape
    Lc = ctx.shape[1]
    depth = w_in.shape[0]
    g = Geom(B, L, Lc, D, depth)
    xi, yi, ci = _coords()
    chip = 2 * xi + yi
    dev = 4 * xi + 2 * yi + ci
    nrow = B + 1

    bf = {n: _ew(lambda t: (t,), [_as2d(A[n])], [BF16], 'cast_' + n)[0].reshape(A[n].shape) for n in BIG}
    names = list(BIG) + list(SMALL_SHARDED)
    gathered = _gather_chips_halved([bf[n] for n in BIG] + [A[n] for n in SMALL_SHARDED], 'gather_weights')
    axis_of = {**BIG, **SMALL_SHARDED}
    full = {n: jnp.concatenate([gt[k] for k in range(N_CHIPS)], axis=axis_of[n]) for n, gt in zip(names, gathered)}

    crow = jnp.concatenate([c, c_ctx[None]], axis=0)
    c_all = _gather_all(crow, 'gather_cond').reshape(N_DEV * nrow, D)
    (a_all,) = _ew(_silu, [c_all], [F32], 'silu_cond')
    ncol = w_ada.shape[2]
    b_ada_sh = lax.dynamic_slice(b_ada, (0, chip * ncol), (depth, ncol))
    m_sh = jnp.stack([_mm(a_all, w_ada[l], 'nn', bias=b_ada_sh[l], name='ada_fwd') for l in range(depth)])
    m_full = _gather_chips_halved([m_sh], 'gather_mods')[0]
    m_full = jnp.concatenate([m_full[k] for k in range(N_CHIPS)], axis=-1)
    mods = lax.dynamic_slice(m_full, (0, dev * nrow, 0), (depth, nrow, 6 * D)).reshape(depth, nrow, 6, D)

    ops = (_build_rowops(g), *_build_seqops(g), _build_attn(B))
    big = {n: full[n] for n in BIG}
    small = {n: (full[n] if n in SMALL_SHARDED else A[n]) for n in SMALL}

    y, vjp = jax.vjp(lambda bg, sm, md, xx: _forward(g, ops, bg, sm, md, xx, ctx), big, small, mods, x)
    dy, loss_part = _loss_call(y, loss_target.reshape(B * L, D))
    loss = lax.psum(loss_part, ('x', 'y', 'c'))
    d_big, d_small, d_mods, grad_x = vjp(dy)

    parts = [jnp.stack(jnp.split(d_big[n], N_CHIPS, axis=BIG[n]), axis=0) for n in BIG]
    recv = _exchange_shards(parts, 'exchange_grads')
    grads = {}
    for n, r in zip(BIG, recv):
        shp = A[n].shape
        s8 = r.reshape((2 * N_CHIPS, -1, shp[-1]))
        grads[n] = _sum_slots(s8, 'sum_' + n).reshape(shp)

    dm_all = _gather_all(d_mods.transpose(1, 0, 2, 3).reshape(nrow, depth * 6 * D), 'gather_dmods')
    dm_all = dm_all.reshape(N_DEV * nrow, depth, 6 * D)
    g_w_ada, g_b_ada, da = [], [], jnp.zeros((N_DEV * nrow, D), F32)
    for l in range(depth):
        dml = dm_all[:, l]
        dml_sh = lax.dynamic_slice(dml, (0, chip * ncol), (N_DEV * nrow, ncol))
        g_w_ada.append(_mm(a_all, dml_sh, 'tn', name='ada_dw'))
        g_b_ada.append(_colsum(dml, name='ada_db')[0])
        da = da + _mm(dml_sh, w_ada[l], 'nt', name='ada_da')
    grads['w_ada'] = jnp.stack(g_w_ada)
    grads['b_ada'] = jnp.stack(g_b_ada)
    (dc_all,) = _ew(_dsilu, [c_all, da], [F32], 'dsilu_cond')
    dc_ctx = dc_all.reshape(N_DEV, nrow, D)[:, B]
    dc_ctx = _sum_slots(dc_ctx.reshape(N_DEV, 1, D), 'sum_dcctx')
    dc_ctx = _gather_chips([jnp.broadcast_to(dc_ctx, (8, D))], 'gather_dcctx')[0]
    grads['c_ctx'] = _sum_slots(dc_ctx, 'sum_dcctx_chips')[0]

    flat = jnp.concatenate([d_small[n].reshape(-1) for n in SMALL])
    n_flat = flat.shape[0]
    pad = (-n_flat) % (128 * 16)
    flat = jnp.pad(flat, (0, pad)).reshape(-1, 128)
    tot = _sum_slots(_gather_all(flat, 'gather_small'), 'sum_small').reshape(-1)
    off = 0
    for n in SMALL:
        shp = d_small[n].shape
        sz = math.prod(shp)
        gn = tot[off:off + sz].reshape(shp)
        off += sz
        if n in SMALL_SHARDED:
            w = A[n].shape[SMALL_SHARDED[n]]
            gn = lax.dynamic_slice_in_dim(gn, chip * w, w, axis=SMALL_SHARDED[n])
        grads[n] = gn

    deltas, new_m, new_v = {}, {}, {}
    for n in W_NAMES:
        deltas[n], new_m[n], new_v[n] = _adamw(A[n], grads[n], A['m_' + n], A['v_' + n], 'adamw_' + n)
    return (loss, grad_x, *[grads[n] for n in W_NAMES], *[deltas[n] for n in W_NAMES],
            *[new_m[n] for n in W_NAMES], *[new_v[n] for n in W_NAMES])
```

```python
import functools
import math

import jax
import jax.numpy as jnp
from jax import lax
from jax.experimental import pallas as pl
from jax.experimental.pallas import tpu as pltpu

F32, BF16 = jnp.float32, jnp.bfloat16
MESH = pl.DeviceIdType.MESH
VMEM_LIMIT_BYTES = 48 * 1024 * 1024
GRID_W = 64
HEAD_DIM = 128
ROPE_THETA = 10000.0
ROPE_PAIRS = HEAD_DIM // 4
POOL_WINDOWS = (2, 4, 8, 16)
LN_EPS = 1e-5
RMS_EPS = 1e-6
ADAM_LR, ADAM_B1, ADAM_B2, ADAM_EPS, ADAM_WD, ADAM_STEP = 0.001, 0.9, 0.999, 1e-08, 0.01, 10
N_CHIPS, N_DEV = 4, 8

W_NAMES = ['c_ctx', 'w_ada', 'b_ada', 'w_in', 'b_in', 'q_gain', 'k_gain', 'conv_dw_w', 'conv_dw_b', 'conv_ln_g',
           'conv_ln_b', 'conv_pw_w', 'conv_pw_b', 'pool_w', 'pool_scale', 'w_out', 'b_out', 'ln1_g', 'ln1_b', 'ln2_g',
           'ln2_b', 'w_up', 'ffn_dw_w', 'ffn_dw_b', 'w_down']
BIG = {'w_in': 2, 'conv_pw_w': 1, 'pool_w': 2, 'w_out': 1, 'w_up': 2, 'w_down': 1}
SMALL_SHARDED = {'conv_dw_w': 2, 'ffn_dw_w': 2}
SMALL = ['b_in', 'q_gain', 'k_gain', 'conv_dw_w', 'conv_dw_b', 'conv_ln_g', 'conv_ln_b', 'conv_pw_b', 'pool_scale',
         'b_out', 'ln1_g', 'ln1_b', 'ln2_g', 'ln2_b', 'ffn_dw_w', 'ffn_dw_b']


def _tile(n, pref, align):
    if n <= pref:
        return n
    t = (pref // align) * align
    while t >= align:
        if n % t == 0:
            return t
        t -= align
    return n


def _call(body, *, name, out_shape, in_specs, out_specs, grid=None, scratch=()):
    kw = {}
    if grid is not None:
        kw['grid'] = grid
        kw['compiler_params'] = pltpu.CompilerParams(dimension_semantics=('arbitrary',) * len(grid),
                                                     vmem_limit_bytes=VMEM_LIMIT_BYTES)
    return pl.pallas_call(body, name=name, out_shape=out_shape, in_specs=in_specs, out_specs=out_specs,
                          scratch_shapes=list(scratch), **kw)


def _mm(a, b, mode, bias=None, out_dtype=F32, name='mm'):
    if mode == 'nn':
        (M, K), (K2, N) = a.shape, b.shape
    elif mode == 'nt':
        (M, K), (N, K2) = a.shape, b.shape
    else:
        (K, M), (K2, N) = a.shape, b.shape
    assert K == K2, (a.shape, b.shape, mode)
    tm = _tile(M, 1024, 128 if mode == 'tn' else 16)
    tn = _tile(N, 1408, 128)
    tk = _tile(K, 1024, 16 if mode == 'tn' else 128)
    nk = K // tk
    if mode == 'nn':
        a_spec = pl.BlockSpec((tm, tk), lambda i, j, k: (i, k))
        b_spec = pl.BlockSpec((tk, tn), lambda i, j, k: (k, j))
        dims = ((1,), (0,))
    elif mode == 'nt':
        a_spec = pl.BlockSpec((tm, tk), lambda i, j, k: (i, k))
        b_spec = pl.BlockSpec((tn, tk), lambda i, j, k: (j, k))
        dims = ((1,), (1,))
    else:
        a_spec = pl.BlockSpec((tk, tm), lambda i, j, k: (k, i))
        b_spec = pl.BlockSpec((tk, tn), lambda i, j, k: (k, j))
        dims = ((0,), (0,))
    has_bias = bias is not None

    def body_single(*refs):
        a_ref, b_ref = refs[0], refs[1]
        r = lax.dot_general(a_ref[...].astype(BF16), b_ref[...].astype(BF16), (dims, ((), ())),
                            preferred_element_type=F32)
        if has_bias:
            r = r + refs[2][...]
        refs[-1][...] = r.astype(out_dtype)

    def body(*refs):
        if has_bias:
            a_ref, b_ref, bias_ref, o_ref, acc = refs
        else:
            a_ref, b_ref, o_ref, acc = refs
        kk = pl.program_id(2)

        @pl.when(kk == 0)
        def _():
            acc[...] = jnp.zeros_like(acc)

        acc[...] += lax.dot_general(a_ref[...].astype(BF16), b_ref[...].astype(BF16), (dims, ((), ())),
                                    preferred_element_type=F32)

        @pl.when(kk == nk - 1)
        def _():
            r = acc[...]
            if has_bias:
                r = r + bias_ref[...]
            o_ref[...] = r.astype(out_dtype)

    in_specs = [a_spec, b_spec]
    args = [a, b]
    if has_bias:
        in_specs.append(pl.BlockSpec((1, tn), lambda i, j, k: (0, j)))
        args.append(bias.reshape(1, N).astype(F32))
    return _call(body_single if nk == 1 else body, name=name, grid=(M // tm, N // tn, nk), in_specs=in_specs,
                 out_specs=pl.BlockSpec((tm, tn), lambda i, j, k: (i, j)),
                 out_shape=jax.ShapeDtypeStruct((M, N), out_dtype),
                 scratch=[] if nk == 1 else [pltpu.VMEM((tm, tn), F32)])(*args)


def _colsum(x, name='colsum'):
    M, N = x.shape
    tm, tn = _tile(M, 1024, 8), _tile(N, 1024, 128)

    def body(x_ref, o_ref):
        @pl.when(pl.program_id(1) == 0)
        def _():
            o_ref[...] = jnp.zeros_like(o_ref)

        o_ref[...] += jnp.sum(x_ref[...].astype(F32), axis=0, keepdims=True)

    return _call(body, name=name, grid=(N // tn, M // tm),
                 in_specs=[pl.BlockSpec((tm, tn), lambda j, i: (i, j))],
                 out_specs=pl.BlockSpec((1, tn), lambda j, i: (0, j)),
                 out_shape=jax.ShapeDtypeStruct((1, N), F32))(x)


@jax.custom_vjp
def linear(x, w, b):
    return _mm(x, w, 'nn', bias=b, name='lin_fwd')


def _linear_fwd(x, w, b):
    return _mm(x, w, 'nn', bias=b, name='lin_fwd'), (x, w)


def _linear_bwd(res, dy):
    x, w = res
    return (_mm(dy, w, 'nt', name='lin_dx'), _mm(x, dy, 'tn', out_dtype=w.dtype, name='lin_dw'),
            _colsum(dy, name='lin_db')[0])


linear.defvjp(_linear_fwd, _linear_bwd)


@jax.custom_vjp
def linear_nb(x, w):
    return _mm(x, w, 'nn', name='linnb_fwd')


def _linear_nb_fwd(x, w):
    return _mm(x, w, 'nn', name='linnb_fwd'), (x, w)


def _linear_nb_bwd(res, dy):
    x, w = res
    return _mm(dy, w, 'nt', name='linnb_dx'), _mm(x, dy, 'tn', out_dtype=w.dtype, name='linnb_dw')


linear_nb.defvjp(_linear_nb_fwd, _linear_nb_bwd)


class Geom:
    def __init__(self, B, L, Lc, D, depth):
        self.B, self.L, self.Lc, self.D, self.depth = B, L, Lc, D, depth
        self.TL = _tile(math.gcd(L, Lc), 256, 8)
        self.tpe, self.tpc = L // self.TL, Lc // self.TL
        self.R = B * (L + Lc)
        self.NT = self.R // self.TL
        self.n_lat_tiles = B * self.tpe
        self.alpha = (2 * depth) ** 0.25

    def pidx(self, i):
        return jnp.minimum(i // self.tpe, self.B)

    def first_of_group(self, i):
        return jnp.where(i < self.n_lat_tiles, (i % self.tpe) == 0, i == self.n_lat_tiles)

    def seqpos(self, i):
        lat = i < self.n_lat_tiles
        pos0 = jnp.where(lat, (i % self.tpe) * self.TL, ((i - self.n_lat_tiles) % self.tpc) * self.TL)
        return pos0, jnp.where(lat, self.L, self.Lc)


def _make_rowop(g, fn, nr, nc, npar, name):
    def specs(rows, consts, params):
        hm = rows[0].shape[0] // g.R
        rs = [pl.BlockSpec((g.TL * hm, a.shape[1]), lambda i: (i, 0)) for a in list(rows) + list(consts)]
        ps = [pl.BlockSpec((None,) + a.shape[1:], lambda i: (g.pidx(i), 0, 0)) for a in params]
        return hm, rs, ps

    def out_struct(rows, consts, params, hm):
        tiles = [jax.ShapeDtypeStruct((g.TL * hm, a.shape[1]), F32) for a in list(rows) + list(consts)]
        tiles += [jax.ShapeDtypeStruct(a.shape[1:], F32) for a in params]
        return jax.eval_shape(fn, *tiles)

    def run_fwd(rows, consts, params):
        hm, rs, ps = specs(rows, consts, params)
        outs = out_struct(rows, consts, params, hm)
        n_in = nr + nc + npar

        def body(*refs):
            vals = [r[...] for r in refs[:n_in]]
            res = fn(*vals)
            for o_ref, v in zip(refs[n_in:], res):
                o_ref[...] = v

        return _call(body, name=name + '_fwd', grid=(g.NT,), in_specs=rs + ps,
                     out_specs=[pl.BlockSpec(o.shape, lambda i: (i, 0)) for o in outs],
                     out_shape=[jax.ShapeDtypeStruct((g.R * hm, o.shape[1]), F32) for o in outs])(
                         *rows, *consts, *params)

    def run_bwd(rows, consts, params, douts):
        hm, rs, ps = specs(rows, consts, params)
        outs = out_struct(rows, consts, params, hm)
        n_in = nr + nc + npar
        n_do = len(outs)

        def body(*refs):
            rv = [r[...] for r in refs[:nr]]
            cv = [r[...] for r in refs[nr:nr + nc]]
            pv = [r[...] for r in refs[nr + nc:n_in]]
            dov = tuple(r[...] for r in refs[n_in:n_in + n_do])
            drow_refs = refs[n_in + n_do:n_in + n_do + nr]
            dpar_refs = refs[n_in + n_do + nr:]

            def f(*d):
                return fn(*d[:nr], *cv, *d[nr:])

            _, vjp = jax.vjp(f, *rv, *pv)
            gr = vjp(dov)
            for r, v in zip(drow_refs, gr[:nr]):
                r[...] = v
            if npar:
                @pl.when(g.first_of_group(pl.program_id(0)))
                def _():
                    for r in dpar_refs:
                        r[...] = jnp.zeros_like(r)

                for r, v in zip(dpar_refs, gr[nr:]):
                    r[...] += v

        do_specs = [pl.BlockSpec(o.shape, lambda i: (i, 0)) for o in outs]
        res = _call(body, name=name + '_bwd', grid=(g.NT,), in_specs=rs + ps + do_specs,
                    out_specs=rs[:nr] + ps,
                    out_shape=[jax.ShapeDtypeStruct(a.shape, F32) for a in list(rows) + list(params)])(
                        *rows, *consts, *params, *douts)
        return tuple(res[:nr]), tuple(res[nr:])

    @jax.custom_vjp
    def op(rows, consts, params):
        return tuple(run_fwd(rows, consts, params))

    def op_fwd(rows, consts, params):
        return tuple(run_fwd(rows, consts, params)), (rows, consts, params)

    def op_bwd(res, douts):
        rows, consts, params = res
        drows, dpars = run_bwd(rows, consts, params, douts)
        return drows, tuple(jnp.zeros_like(c) for c in consts), dpars

    op.defvjp(op_fwd, op_bwd)
    return op


def _lane_iota(shape):
    return lax.broadcasted_iota(jnp.int32, shape, len(shape) - 1)


@jax.custom_vjp
def _swap_halves(t):
    w = t.shape[-1]
    first = (_lane_iota(t.shape) % 64) < 32
    return jnp.where(first, pltpu.roll(t, w - 32, t.ndim - 1), pltpu.roll(t, 32, t.ndim - 1))


_swap_halves.defvjp(lambda t: (_swap_halves(t), None), lambda _, ct: (_swap_halves(ct),))


def _ln(t, gain, bias):
    mu = jnp.mean(t, axis=-1, keepdims=True)
    d = t - mu
    var = jnp.mean(d * d, axis=-1, keepdims=True)
    return d * lax.rsqrt(var + LN_EPS) * gain + bias


def _sig(t):
    return jax.nn.sigmoid(t)


def _build_rowops(g):
    alpha = g.alpha

    def f_mod(x, shift, scale):
        return (x * (1.0 + scale) + shift,)

    def f_qk(z, cos, sin, gain):
        ms = jnp.mean(z * z, axis=-1, keepdims=True)
        t = z * lax.rsqrt(ms + RMS_EPS) * gain
        return (t * cos + _swap_halves(t) * sin,)

    def f_glu(a, gt):
        return (a * _sig(gt),)

    def f_lnsilu(c1, gain, bias):
        y = _ln(c1, gain, bias)
        return (y * _sig(y),)

    def f_merge(g0, g1, g2, at, co, mx, ps):
        return (_sig(g0) * at + _sig(g1) * co + _sig(g2) * (mx * ps),)

    def f_resln(x, y, gate, gain, bias):
        return (_ln(alpha * x + gate * y, gain, bias),)

    def f_silumul(a, u):
        return (a * _sig(a) * u,)

    mk = functools.partial(_make_rowop, g)
    return dict(mod=mk(f_mod, 1, 0, 2, 'mod'), qk=mk(f_qk, 1, 2, 1, 'qk'), glu=mk(f_glu, 2, 0, 0, 'glu'),
                lnsilu=mk(f_lnsilu, 1, 0, 2, 'lnsilu'), merge=mk(f_merge, 6, 0, 1, 'merge'),
                resln=mk(f_resln, 2, 0, 3, 'resln'), silumul=mk(f_silumul, 2, 0, 0, 'silumul'))


HALO = 16
LANES = 128


def _halo_specs(g, tc, rows_first=True):
    r = g.TL // HALO
    last = g.R // HALO - 1

    def mk(shape, fn):
        return pl.BlockSpec(shape, fn if rows_first else (lambda j, i: fn(i, j)))

    return [mk((HALO, tc), lambda i, j: (jnp.maximum(i * r - 1, 0), j)),
            mk((g.TL, tc), lambda i, j: (i, j)),
            mk((HALO, tc), lambda i, j: (jnp.minimum((i + 1) * r, last), j))]


def _extended(g, i, top, mid, bot):
    pos0, ls = g.seqpos(i)
    top = jnp.where(pos0 > 0, top, 0.0)
    bot = jnp.where(pos0 + g.TL < ls, bot, 0.0)
    return jnp.concatenate([top, mid, bot], axis=0)


class _Taps:
    def __init__(self, ext, TL):
        self.ext, self.TL, self.rolled = ext, TL, {0: ext}

    def __call__(self, off):
        a, r = divmod(-off, 8)
        if r not in self.rolled:
            self.rolled[r] = pltpu.roll(self.ext, r, 0)
        start = HALO - 8 * a
        return self.rolled[r][start:start + self.TL]


def _conv_call(g, x, w, b, name):
    K, C = w.shape
    p = (K - 1) // 2
    tc = _tile(C, 512, LANES)
    TL = g.TL
    assert p <= HALO and TL % HALO == 0 and tc % LANES == 0

    def body(xp_ref, xc_ref, xn_ref, w_ref, b_ref, o_ref):
        i = pl.program_id(0)
        for c0 in range(0, tc, LANES):
            sl = slice(c0, c0 + LANES)
            ext = _extended(g, i, xp_ref[:, sl], xc_ref[:, sl], xn_ref[:, sl])
            taps = _Taps(ext, TL)
            acc = jnp.zeros((TL, LANES), F32) + b_ref[:, sl]
            for k in range(K):
                acc = acc + taps(k - p) * w_ref[pl.ds(k, 1), sl]
            o_ref[:, sl] = acc

    return _call(body, name=name, grid=(g.NT, C // tc),
                 in_specs=_halo_specs(g, tc) + [pl.BlockSpec((K, tc), lambda i, j: (0, j)),
                                                pl.BlockSpec((1, tc), lambda i, j: (0, j))],
                 out_specs=pl.BlockSpec((TL, tc), lambda i, j: (i, j)),
                 out_shape=jax.ShapeDtypeStruct(x.shape, F32))(x, x, x, w, b.reshape(1, C))


def _conv_dw_call(g, x, dy, K, name):
    C = x.shape[1]
    p = (K - 1) // 2
    tc = _tile(C, 512, LANES)
    TL = g.TL
    assert p <= HALO and TL % HALO == 0 and tc % LANES == 0

    def body(xp_ref, xc_ref, xn_ref, dy_ref, dw_ref, db_ref):
        i = pl.program_id(1)

        @pl.when(i == 0)
        def _():
            dw_ref[...] = jnp.zeros_like(dw_ref)
            db_ref[...] = jnp.zeros_like(db_ref)

        for c0 in range(0, tc, LANES):
            sl = slice(c0, c0 + LANES)
            ext = _extended(g, i, xp_ref[:, sl], xc_ref[:, sl], xn_ref[:, sl])
            taps = _Taps(ext, TL)
            dy_v = dy_ref[:, sl]
            db_ref[:, sl] += jnp.sum(dy_v, axis=0, keepdims=True)
            for k in range(K):
                dw_ref[pl.ds(k, 1), sl] += jnp.sum(taps(k - p) * dy_v, axis=0, keepdims=True)

    return _call(body, name=name, grid=(C // tc, g.NT),
                 in_specs=_halo_specs(g, tc, rows_first=False) + [pl.BlockSpec((TL, tc), lambda j, i: (i, j))],
                 out_specs=[pl.BlockSpec((K, tc), lambda j, i: (0, j)), pl.BlockSpec((1, tc), lambda j, i: (0, j))],
                 out_shape=[jax.ShapeDtypeStruct((K, C), F32), jax.ShapeDtypeStruct((1, C), F32)])(x, x, x, dy)


def _pool_call(g, u, transpose, name):
    C = u.shape[1]
    cg = C // len(POOL_WINDOWS)
    TL = g.TL
    assert max(POOL_WINDOWS) // 2 <= HALO and TL % HALO == 0 and cg % LANES == 0

    def body(up_ref, uc_ref, un_ref, o_ref):
        i = pl.program_id(0)
        pos0, ls = g.seqpos(i)
        grp = pl.program_id(1)
        pos_e = pos0 - HALO + lax.broadcasted_iota(jnp.int32, (TL + 2 * HALO, LANES), 0)
        for gi, wdw in enumerate(POOL_WINDOWS):
            @pl.when(grp == gi)
            def _(wdw=wdw):
                half = wdw // 2
                cnt = jnp.minimum(pos_e - half + wdw, ls) - jnp.maximum(pos_e - half, 0)
                cnt = jnp.maximum(cnt, 1).astype(F32)
                for c0 in range(0, cg, LANES):
                    sl = slice(c0, c0 + LANES)
                    ext = _extended(g, i, up_ref[:, sl], uc_ref[:, sl], un_ref[:, sl])
                    if transpose:
                        ext = ext / cnt
                        offs = range(-half + 1, half + 1)
                    else:
                        offs = range(-half, half)
                    taps = _Taps(ext, TL)
                    acc = jnp.zeros((TL, LANES), F32)
                    for off in offs:
                        acc = acc + taps(off)
                    if transpose:
                        o_ref[:, sl] = acc - uc_ref[:, sl]
                    else:
                        o_ref[:, sl] = acc / cnt[HALO:HALO + TL] - uc_ref[:, sl]

    return _call(body, name=name, grid=(g.NT, len(POOL_WINDOWS)), in_specs=_halo_specs(g, cg),
                 out_specs=pl.BlockSpec((TL, cg), lambda i, j: (i, j)),
                 out_shape=jax.ShapeDtypeStruct(u.shape, F32))(u, u, u)


def _build_seqops(g):
    @jax.custom_vjp
    def dwconv(x, w, b):
        return _conv_call(g, x, w, b, 'conv_fwd')

    def dwconv_fwd(x, w, b):
        return _conv_call(g, x, w, b, 'conv_fwd'), (x, w)

    def dwconv_bwd(res, dy):
        x, w = res
        dx = _conv_call(g, dy, w[::-1], jnp.zeros((w.shape[1],), F32), 'conv_dx')
        dw, db = _conv_dw_call(g, x, dy, w.shape[0], 'conv_dw')
        return dx, dw, db[0]

    dwconv.defvjp(dwconv_fwd, dwconv_bwd)

    @jax.custom_vjp
    def pool(u):
        return _pool_call(g, u, False, 'pool_fwd')

    pool.defvjp(lambda u: (_pool_call(g, u, False, 'pool_fwd'), None),
                lambda _, dp: (_pool_call(g, dp, True, 'pool_bwd'),))
    return dwconv, pool


def _attn_probs(qg, ks, scale):
    ss = [lax.dot_general(qg, k, (((1,), (1,)), ((), ())), preferred_element_type=F32) * scale for k in ks]
    m = ss[0].max(axis=-1, keepdims=True)
    for s in ss[1:]:
        m = jnp.maximum(m, s.max(axis=-1, keepdims=True))
    es = [jnp.exp(s - m) for s in ss]
    den = es[0].sum(axis=-1, keepdims=True)
    for e in es[1:]:
        den = den + e.sum(axis=-1, keepdims=True)
    inv = 1.0 / den
    return [e * inv for e in es]


def _attn_specs(B, q, kvs):
    lq = q.shape[0] // B
    kvh = kvs[0].shape[1] // HEAD_DIM
    grp = q.shape[1] // HEAD_DIM // kvh
    tq = _tile(lq, 256, 8)
    nq = lq // tq
    q_spec = pl.BlockSpec((tq, grp * HEAD_DIM), lambda b, h, i: (b * nq + i, h))
    kv_specs = [pl.BlockSpec((a.shape[0] // B, HEAD_DIM), lambda b, h, i: (b, h)) for a in kvs]
    return kvh, grp, tq, nq, q_spec, kv_specs


def _attn_fwd_call(B, q, kvs, name):
    kvh, grp, tq, nq, q_spec, kv_specs = _attn_specs(B, q, kvs)
    ns = len(kvs) // 2
    scale = HEAD_DIM ** -0.5

    def body(*refs):
        q_ref, o_ref = refs[0], refs[-1]
        ks = [refs[1 + 2 * s][...].astype(BF16) for s in range(ns)]
        vs = [refs[2 + 2 * s][...].astype(BF16) for s in range(ns)]
        for gi in range(grp):
            sl = slice(gi * HEAD_DIM, (gi + 1) * HEAD_DIM)
            ps = _attn_probs(q_ref[:, sl].astype(BF16), ks, scale)
            o = jnp.zeros((tq, HEAD_DIM), F32)
            for p, v in zip(ps, vs):
                o = o + jnp.dot(p.astype(BF16), v, preferred_element_type=F32)
            o_ref[:, sl] = o

    return _call(body, name=name, grid=(B, kvh, nq), in_specs=[q_spec] + kv_specs, out_specs=q_spec,
                 out_shape=jax.ShapeDtypeStruct(q.shape, F32))(q, *kvs)


def _attn_bwd_call(B, q, kvs, do, name):
    kvh, grp, tq, nq, q_spec, kv_specs = _attn_specs(B, q, kvs)
    ns = len(kvs) // 2
    scale = HEAD_DIM ** -0.5

    def body(*refs):
        q_ref, do_ref = refs[0], refs[1 + 2 * ns]
        dq_ref = refs[2 + 2 * ns]
        dkv_refs = refs[3 + 2 * ns:]
        ks = [refs[1 + 2 * s][...].astype(BF16) for s in range(ns)]
        vs = [refs[2 + 2 * s][...].astype(BF16) for s in range(ns)]

        @pl.when(pl.program_id(2) == 0)
        def _():
            for r in dkv_refs:
                r[...] = jnp.zeros_like(r)

        for gi in range(grp):
            sl = slice(gi * HEAD_DIM, (gi + 1) * HEAD_DIM)
            qg = q_ref[:, sl].astype(BF16)
            dog = do_ref[:, sl].astype(BF16)
            ps = _attn_probs(qg, ks, scale)
            dps = [lax.dot_general(dog, v, (((1,), (1,)), ((), ())), preferred_element_type=F32) for v in vs]
            delta = (ps[0] * dps[0]).sum(axis=-1, keepdims=True)
            for p, dp in zip(ps[1:], dps[1:]):
                delta = delta + (p * dp).sum(axis=-1, keepdims=True)
            dq = jnp.zeros((tq, HEAD_DIM), F32)
            for s in range(ns):
                ds = (ps[s] * (dps[s] - delta) * scale).astype(BF16)
                dq = dq + jnp.dot(ds, ks[s], preferred_element_type=F32)
                dkv_refs[2 * s][...] += lax.dot_general(ds, qg, (((0,), (0,)), ((), ())),
                                                        preferred_element_type=F32)
                dkv_refs[2 * s + 1][...] += lax.dot_general(ps[s].astype(BF16), dog, (((0,), (0,)), ((), ())),
                                                            preferred_element_type=F32)
            dq_ref[:, sl] = dq

    res = _call(body, name=name, grid=(B, kvh, nq), in_specs=[q_spec] + kv_specs + [q_spec],
                out_specs=[q_spec] + kv_specs,
                out_shape=[jax.ShapeDtypeStruct(a.shape, F32) for a in [q] + list(kvs)])(q, *kvs, do)
    return tuple(res)


def _build_attn(B):
    @jax.custom_vjp
    def attn(q, kvs):
        return _attn_fwd_call(B, q, kvs, 'attn%d_fwd' % (len(kvs) // 2))

    def attn_fwd(q, kvs):
        return attn(q, kvs), (q, kvs)

    def attn_bwd(res, do):
        q, kvs = res
        out = _attn_bwd_call(B, q, kvs, do, 'attn%d_bwd' % (len(kvs) // 2))
        return out[0], tuple(out[1:])

    attn.defvjp(attn_fwd, attn_bwd)
    return attn


def _ew(fn, arrays, out_dtypes, name):
    rows, cols = arrays[0].shape
    tr = _tile(rows, 256, 16)
    n_in = len(arrays)

    def body(*refs):
        res = fn(*[r[...] for r in refs[:n_in]])
        for r, v in zip(refs[n_in:], res):
            r[...] = v.astype(r.dtype)

    spec = pl.BlockSpec((tr, cols), lambda i: (i, 0))
    return _call(body, name=name, grid=(rows // tr,), in_specs=[spec] * n_in, out_specs=[spec] * len(out_dtypes),
                 out_shape=[jax.ShapeDtypeStruct((rows, cols), d) for d in out_dtypes])(*arrays)


def _as2d(a):
    return a.reshape(-1, a.shape[-1]) if a.ndim > 1 else a.reshape(1, -1)


def _sum_slots(x, name):
    n, rows, cols = x.shape
    tr = _tile(rows, 256, 16)

    def body(x_ref, o_ref):
        acc = x_ref[0].astype(F32)
        for s in range(1, n):
            acc = acc + x_ref[s].astype(F32)
        o_ref[...] = acc

    return _call(body, name=name, grid=(rows // tr,), in_specs=[pl.BlockSpec((n, tr, cols), lambda i: (0, i, 0))],
                 out_specs=pl.BlockSpec((tr, cols), lambda i: (i, 0)),
                 out_shape=jax.ShapeDtypeStruct((rows, cols), F32))(x)


def _adamw(w, gr, m, v, name):
    b1c = 1.0 - ADAM_B1 ** ADAM_STEP
    b2c = 1.0 - ADAM_B2 ** ADAM_STEP

    def fn(w_, g_, m_, v_):
        mn = ADAM_B1 * m_ + (1.0 - ADAM_B1) * g_
        vn = ADAM_B2 * v_ + (1.0 - ADAM_B2) * (g_ * g_)
        delta = -ADAM_LR * ((mn / b1c) / (jnp.sqrt(vn / b2c) + ADAM_EPS) + ADAM_WD * w_)
        return delta, mn, vn

    shp = w.shape
    outs = _ew(fn, [_as2d(a) for a in (w, gr, m, v)], [F32, F32, F32], name)
    return [o.reshape(shp) for o in outs]


def _loss_call(y, target):
    rows, D = y.shape
    tr = _tile(rows, 512, 8)

    def body(y_ref, t_ref, dy_ref, l_ref):
        @pl.when(pl.program_id(0) == 0)
        def _():
            l_ref[...] = jnp.zeros_like(l_ref)

        e = y_ref[...] - t_ref[...]
        dy_ref[...] = e * (1.0 / D)
        s = jnp.sum(jnp.sum(e * e, axis=1, keepdims=True), axis=0, keepdims=True)
        l_ref[...] += s * (0.5 / D)

    spec = pl.BlockSpec((tr, D), lambda i: (i, 0))
    dy, lsum = _call(body, name='loss', grid=(rows // tr,), in_specs=[spec, spec],
                     out_specs=[spec, pl.BlockSpec((1, 128), lambda i: (0, 0))],
                     out_shape=[jax.ShapeDtypeStruct((rows, D), F32), jax.ShapeDtypeStruct((1, 128), F32)])(y, target)
    return dy, lsum[0, 0]


ANY = pl.BlockSpec(memory_space=pl.ANY)


def _coords():
    return lax.axis_index('x'), lax.axis_index('y'), lax.axis_index('c')


def _gather_chips(arrs, name):
    n = len(arrs)
    rel = [(1, 0), (0, 1), (1, 1)]

    def body(*refs):
        ins, outs = refs[:n], refs[n:2 * n]
        send, recv, loc = refs[2 * n:]
        x, y, c = _coords()
        me = 2 * x + y
        local = [pltpu.make_async_copy(ins[a], outs[a].at[me], loc.at[a]) for a in range(n)]
        for cp in local:
            cp.start()
        remote = []
        for a in range(n):
            for k, (dx, dy) in enumerate(rel):
                cp = pltpu.make_async_remote_copy(src_ref=ins[a], dst_ref=outs[a].at[me], send_sem=send.at[a * 3 + k],
                                                  recv_sem=recv.at[a * 3 + k],
                                                  device_id=((x + dx) % 2, (y + dy) % 2, c), device_id_type=MESH)
                cp.start()
                remote.append(cp)
        for cp in remote:
            cp.wait()
        for cp in local:
            cp.wait()

    return pl.pallas_call(body, name=name, in_specs=[ANY] * n, out_specs=[ANY] * n,
                          out_shape=[jax.ShapeDtypeStruct((N_CHIPS,) + a.shape, a.dtype) for a in arrs],
                          scratch_shapes=[pltpu.SemaphoreType.DMA((3 * n,)), pltpu.SemaphoreType.DMA((3 * n,)),
                                          pltpu.SemaphoreType.DMA((n,))])(*arrs)


def _gather_chips_halved(arrs, name):
    n = len(arrs)
    rel = [(1, 0), (0, 1), (1, 1)]
    for a in arrs:
        assert a.ndim >= 3 and a.shape[0] % 2 == 0, a.shape

    def body(*refs):
        ins, outs = refs[:n], refs[n:2 * n]
        send, recv, fsend, frecv, loc = refs[2 * n:]
        x, y, c = _coords()
        me = 2 * x + y
        local = [pltpu.make_async_copy(ins[a], outs[a].at[me], loc.at[a]) for a in range(n)]
        for cp in local:
            cp.start()
        first, passed = [], []
        for a in range(n):
            hn = arrs[a].shape[0] // 2
            mine, theirs = pl.ds(c * hn, hn), pl.ds((1 - c) * hn, hn)
            for k, (dx, dy) in enumerate(rel):
                tx, ty = (x + dx) % 2, (y + dy) % 2
                src = 2 * tx + ty
                s = a * 3 + k
                first.append(pltpu.make_async_remote_copy(
                    src_ref=ins[a].at[mine], dst_ref=outs[a].at[me, mine], send_sem=send.at[s], recv_sem=recv.at[s],
                    device_id=(tx, ty, c), device_id_type=MESH))
                passed.append((
                    pltpu.make_async_remote_copy(
                        src_ref=outs[a].at[src, mine], dst_ref=outs[a].at[src, mine], send_sem=fsend.at[s],
                        recv_sem=frecv.at[s], device_id=(x, y, 1 - c), device_id_type=MESH),
                    pltpu.make_async_remote_copy(
                        src_ref=outs[a].at[src, theirs], dst_ref=outs[a].at[src, theirs], send_sem=fsend.at[s],
                        recv_sem=frecv.at[s], device_id=(x, y, 1 - c), device_id_type=MESH),
                    pltpu.make_async_remote_copy(
                        src_ref=ins[a].at[mine], dst_ref=outs[a].at[src, mine], send_sem=send.at[s],
                        recv_sem=recv.at[s], device_id=(tx, ty, c), device_id_type=MESH)))
        for cp in first:
            cp.start()
        for fwd, _, landed in passed:
            landed.wait_recv()
            fwd.start()
        for cp in first:
            cp.wait_send()
        for fwd, from_sibling, _ in passed:
            fwd.wait_send()
            from_sibling.wait_recv()
        for cp in local:
            cp.wait()

    return pl.pallas_call(body, name=name, in_specs=[ANY] * n, out_specs=[ANY] * n,
                          out_shape=[jax.ShapeDtypeStruct((N_CHIPS,) + a.shape, a.dtype) for a in arrs],
                          scratch_shapes=[pltpu.SemaphoreType.DMA((3 * n,)), pltpu.SemaphoreType.DMA((3 * n,)),
                                          pltpu.SemaphoreType.DMA((3 * n,)), pltpu.SemaphoreType.DMA((3 * n,)),
                                          pltpu.SemaphoreType.DMA((n,))])(*arrs)


def _gather_all(arr, name):
    def body(in_ref, out_ref, send, recv, loc):
        x, y, c = _coords()
        me = 4 * x + 2 * y + c
        local = pltpu.make_async_copy(in_ref, out_ref.at[me], loc.at[0])
        local.start()
        remote = []
        for k in range(1, N_DEV):
            dx, dy, dc = (k >> 2) & 1, (k >> 1) & 1, k & 1
            cp = pltpu.make_async_remote_copy(src_ref=in_ref, dst_ref=out_ref.at[me], send_sem=send.at[k - 1],
                                              recv_sem=recv.at[k - 1],
                                              device_id=((x + dx) % 2, (y + dy) % 2, (c + dc) % 2), device_id_type=MESH)
            cp.start()
            remote.append(cp)
        for cp in remote:
            cp.wait()
        local.wait()

    return pl.pallas_call(body, name=name, in_specs=[ANY], out_specs=ANY,
                          out_shape=jax.ShapeDtypeStruct((N_DEV,) + arr.shape, arr.dtype),
                          scratch_shapes=[pltpu.SemaphoreType.DMA((N_DEV - 1,)), pltpu.SemaphoreType.DMA((N_DEV - 1,)),
                                          pltpu.SemaphoreType.DMA((1,))])(arr)


def _exchange_shards(arrs, name):
    n = len(arrs)
    rel = [(1, 0), (0, 1), (1, 1)]

    def body(*refs):
        ins, outs = refs[:n], refs[n:2 * n]
        send, recv, loc, sib_send, sib_recv = refs[2 * n:]
        x, y, c = _coords()
        me = 2 * x + y
        local = [pltpu.make_async_copy(ins[a].at[me], outs[a].at[c, me], loc.at[a]) for a in range(n)]
        for cp in local:
            cp.start()
        remote = []
        for a in range(n):
            for k, (dx, dy) in enumerate(rel):
                tx, ty = (x + dx) % 2, (y + dy) % 2
                cp = pltpu.make_async_remote_copy(src_ref=ins[a].at[2 * tx + ty], dst_ref=outs[a].at[c, me],
                                                  send_sem=send.at[a * 3 + k], recv_sem=recv.at[a * 3 + k],
                                                  device_id=(tx, ty, c), device_id_type=MESH)
                cp.start()
                remote.append(cp)
        for cp in remote:
            cp.wait()
        for cp in local:
            cp.wait()
        sib = [pltpu.make_async_remote_copy(src_ref=outs[a].at[c], dst_ref=outs[a].at[c], send_sem=sib_send.at[a],
                                            recv_sem=sib_recv.at[a], device_id=(x, y, 1 - c), device_id_type=MESH)
               for a in range(n)]
        for cp in sib:
            cp.start()
        for a in range(n):
            sib[a].wait_send()
            pltpu.make_async_remote_copy(src_ref=outs[a].at[1 - c], dst_ref=outs[a].at[1 - c], send_sem=sib_send.at[a],
                                         recv_sem=sib_recv.at[a], device_id=(x, y, 1 - c),
                                         device_id_type=MESH).wait_recv()

    return pl.pallas_call(body, name=name, in_specs=[ANY] * n, out_specs=[ANY] * n,
                          out_shape=[jax.ShapeDtypeStruct((2,) + a.shape, a.dtype) for a in arrs],
                          scratch_shapes=[pltpu.SemaphoreType.DMA((3 * n,)), pltpu.SemaphoreType.DMA((3 * n,)),
                                          pltpu.SemaphoreType.DMA((n,)), pltpu.SemaphoreType.DMA((n,)),
                                          pltpu.SemaphoreType.DMA((n,))])(*arrs)


def _rope_tables(g, heads):
    t = jnp.arange(g.L, dtype=F32)
    row, col = jnp.floor(t / GRID_W), t % GRID_W
    inv = ROPE_THETA ** (-jnp.arange(ROPE_PAIRS, dtype=F32) / ROPE_PAIRS)
    ang = jnp.stack([row, col], axis=-1)[..., None] * inv
    cos = jnp.broadcast_to(jnp.cos(ang)[:, :, None, :], (g.L, 2, 2, ROPE_PAIRS)).reshape(g.L, HEAD_DIM)
    sin = jnp.sin(ang)[:, :, None, :] * jnp.array([-1.0, 1.0], F32)[None, None, :, None]
    sin = sin.reshape(g.L, HEAD_DIM)
    n_ctx = g.B * g.Lc
    cos = jnp.concatenate([jnp.tile(cos, (g.B, 1)), jnp.ones((n_ctx, HEAD_DIM), F32)])
    sin = jnp.concatenate([jnp.tile(sin, (g.B, 1)), jnp.zeros((n_ctx, HEAD_DIM), F32)])
    return jnp.repeat(cos, heads, axis=0), jnp.repeat(sin, heads, axis=0)


def _forward(g, ops, big, small, mods, x, ctx):
    D, B = g.D, g.B
    rop, dwconv, pool, attn = ops
    nh = D // HEAD_DIM
    nkv = nh // 4
    kvw = nkv * HEAD_DIM
    q_end, k_end, v_end = D, D + kvw, D + 2 * kvw
    conv_end = v_end + 2 * D
    pool_end = conv_end + D
    cg = D // len(POOL_WINDOWS)
    n_lat = B * g.L
    cos_q, sin_q = _rope_tables(g, nh)
    cos_k, sin_k = _rope_tables(g, nkv)
    xa = jnp.concatenate([x.reshape(n_lat, D), ctx.reshape(B * g.Lc, D)])

    def shared(v):
        return jnp.broadcast_to(v[None, None, :], (B + 1, 1, v.shape[0]))

    for l in range(g.depth):
        md = [mods[l, :, j][:, None, :] for j in range(6)]
        w_in, b_in = big['w_in'][l], small['b_in'][l]

        def proj(h, a, b):
            return linear(h, w_in[:, a:b], b_in[a:b])

        (h,) = rop['mod']((xa,), (), (md[0], md[1]))
        zq, zk, zv = proj(h, 0, q_end), proj(h, q_end, k_end), proj(h, k_end, v_end)
        (qh,) = rop['qk']((zq.reshape(-1, HEAD_DIM),), (cos_q, sin_q), (shared(small['q_gain'][l]),))
        (kh,) = rop['qk']((zk.reshape(-1, HEAD_DIM),), (cos_k, sin_k), (shared(small['k_gain'][l]),))
        qh, kh = qh.reshape(-1, D), kh.reshape(-1, kvw)
        o_lat = attn(qh[:n_lat], (kh[n_lat:], zv[n_lat:], kh[:n_lat], zv[:n_lat]))
        o_ctx = attn(qh[n_lat:], (kh[n_lat:], zv[n_lat:]))
        attn_o = jnp.concatenate([o_lat, o_ctx])

        (glu,) = rop['glu']((proj(h, v_end, v_end + D), proj(h, v_end + D, conv_end)), (), ())
        c1 = dwconv(glu, small['conv_dw_w'][l], small['conv_dw_b'][l])
        (c2,) = rop['lnsilu']((c1,), (), (shared(small['conv_ln_g'][l]), shared(small['conv_ln_b'][l])))
        conv_o = linear(c2, big['conv_pw_w'][l], small['conv_pw_b'][l])

        pooled = pool(proj(h, conv_end, pool_end))
        w_bd = jnp.zeros((D, D), BF16)
        for gi in range(len(POOL_WINDOWS)):
            w_bd = lax.dynamic_update_slice(w_bd, big['pool_w'][l, gi], (gi * cg, gi * cg))
        mixed = linear_nb(pooled, w_bd)

        gates = [proj(h, pool_end + j * D, pool_end + (j + 1) * D) for j in range(3)]
        (m,) = rop['merge']((gates[0], gates[1], gates[2], attn_o, conv_o, mixed), (),
                            (shared(small['pool_scale'][l]),))
        y = linear(m, big['w_out'][l], small['b_out'][l])
        (x1,) = rop['resln']((xa, y), (), (md[2], shared(small['ln1_g'][l]), shared(small['ln1_b'][l])))

        (h2,) = rop['mod']((x1,), (), (md[3], md[4]))
        dff = big['w_up'].shape[2] // 2
        ua = linear_nb(h2, big['w_up'][l][:, :dff])
        uu = linear_nb(h2, big['w_up'][l][:, dff:])
        a = dwconv(ua, small['ffn_dw_w'][l], small['ffn_dw_b'][l])
        (f,) = rop['silumul']((a, uu), (), ())
        y2 = linear_nb(f, big['w_down'][l])
        (xa,) = rop['resln']((x1, y2), (), (md[5], shared(small['ln2_g'][l]), shared(small['ln2_b'][l])))
    return xa[:n_lat]


def _silu(z):
    return (z * _sig(z),)


def _dsilu(z, d):
    s = _sig(z)
    return (d * s * (1.0 + z * (1.0 - s)),)


def kernel(x, c, ctx, c_ctx, w_ada, b_ada, w_in, b_in, q_gain, k_gain, conv_dw_w, conv_dw_b, conv_ln_g, conv_ln_b, conv_pw_w, conv_pw_b, pool_w, pool_scale, w_out, b_out, ln1_g, ln1_b, ln2_g, ln2_b, w_up, ffn_dw_w, ffn_dw_b, w_down, loss_target, m_c_ctx, m_w_ada, m_b_ada, m_w_in, m_b_in, m_q_gain, m_k_gain, m_conv_dw_w, m_conv_dw_b, m_conv_ln_g, m_conv_ln_b, m_conv_pw_w, m_conv_pw_b, m_pool_w, m_pool_scale, m_w_out, m_b_out, m_ln1_g, m_ln1_b, m_ln2_g, m_ln2_b, m_w_up, m_ffn_dw_w, m_ffn_dw_b, m_w_down, v_c_ctx, v_w_ada, v_b_ada, v_w_in, v_b_in, v_q_gain, v_k_gain, v_conv_dw_w, v_conv_dw_b, v_conv_ln_g, v_conv_ln_b, v_conv_pw_w, v_conv_pw_b, v_pool_w, v_pool_scale, v_w_out, v_b_out, v_ln1_g, v_ln1_b, v_ln2_g, v_ln2_b, v_w_up, v_ffn_dw_w, v_ffn_dw_b, v_w_down):
    A = dict(locals())
    B, L, D = x.shape
    Lc = ctx.shape[1]
    depth = w_in.shape[0]
    g = Geom(B, L, Lc, D, depth)
    xi, yi, ci = _coords()
    chip = 2 * xi + yi
    dev = 4 * xi + 2 * yi + ci
    nrow = B + 1

    bf = {n: _ew(lambda t: (t,), [_as2d(A[n])], [BF16], 'cast_' + n)[0].reshape(A[n].shape) for n in BIG}
    names = list(BIG) + list(SMALL_SHARDED)
    gathered = _gather_chips_halved([bf[n] for n in BIG] + [A[n] for n in SMALL_SHARDED], 'gather_weights')
    axis_of = {**BIG, **SMALL_SHARDED}
    full = {n: jnp.concatenate([gt[k] for k in range(N_CHIPS)], axis=axis_of[n]) for n, gt in zip(names, gathered)}

    crow = jnp.concatenate([c, c_ctx[None]], axis=0)
    c_all = _gather_all(crow, 'gather_cond').reshape(N_DEV * nrow, D)
    (a_all,) = _ew(_silu, [c_all], [F32], 'silu_cond')
    ncol = w_ada.shape[2]
    b_ada_sh = lax.dynamic_slice(b_ada, (0, chip * ncol), (depth, ncol))
    m_sh = jnp.stack([_mm(a_all, w_ada[l], 'nn', bias=b_ada_sh[l], name='ada_fwd') for l in range(depth)])
    m_full = _gather_chips_halved([m_sh], 'gather_mods')[0]
    m_full = jnp.concatenate([m_full[k] for k in range(N_CHIPS)], axis=-1)
    mods = lax.dynamic_slice(m_full, (0, dev * nrow, 0), (depth, nrow, 6 * D)).reshape(depth, nrow, 6, D)

    ops = (_build_rowops(g), *_build_seqops(g), _build_attn(B))
    big = {n: full[n] for n in BIG}
    small = {n: (full[n] if n in SMALL_SHARDED else A[n]) for n in SMALL}

    y, vjp = jax.vjp(lambda bg, sm, md, xx: _forward(g, ops, bg, sm, md, xx, ctx), big, small, mods, x)
    dy, loss_part = _loss_call(y, loss_target.reshape(B * L, D))
    loss = lax.psum(loss_part, ('x', 'y', 'c'))
    d_big, d_small, d_mods, grad_x = vjp(dy)

    parts = [jnp.stack(jnp.split(d_big[n], N_CHIPS, axis=BIG[n]), axis=0) for n in BIG]
    recv = _exchange_shards(parts, 'exchange_grads')
    grads = {}
    for n, r in zip(BIG, recv):
        shp = A[n].shape
        s8 = r.reshape((2 * N_CHIPS, -1, shp[-1]))
        grads[n] = _sum_slots(s8, 'sum_' + n).reshape(shp)

    dm_all = _gather_all(d_mods.transpose(1, 0, 2, 3).reshape(nrow, depth * 6 * D), 'gather_dmods')
    dm_all = dm_all.reshape(N_DEV * nrow, depth, 6 * D)
    g_w_ada, g_b_ada, da = [], [], jnp.zeros((N_DEV * nrow, D), F32)
    for l in range(depth):
        dml = dm_all[:, l]
        dml_sh = lax.dynamic_slice(dml, (0, chip * ncol), (N_DEV * nrow, ncol))
        g_w_ada.append(_mm(a_all, dml_sh, 'tn', name='ada_dw'))
        g_b_ada.append(_colsum(dml, name='ada_db')[0])
        da = da + _mm(dml_sh, w_ada[l], 'nt', name='ada_da')
    grads['w_ada'] = jnp.stack(g_w_ada)
    grads['b_ada'] = jnp.stack(g_b_ada)
    (dc_all,) = _ew(_dsilu, [c_all, da], [F32], 'dsilu_cond')
    dc_ctx = dc_all.reshape(N_DEV, nrow, D)[:, B]
    dc_ctx = _sum_slots(dc_ctx.reshape(N_DEV, 1, D), 'sum_dcctx')
    dc_ctx = _gather_chips([jnp.broadcast_to(dc_ctx, (8, D))], 'gather_dcctx')[0]
    grads['c_ctx'] = _sum_slots(dc_ctx, 'sum_dcctx_chips')[0]

    flat = jnp.concatenate([d_small[n].reshape(-1) for n in SMALL])
    n_flat = flat.shape[0]
    pad = (-n_flat) % (128 * 16)
    flat = jnp.pad(flat, (0, pad)).reshape(-1, 128)
    tot = _sum_slots(_gather_all(flat, 'gather_small'), 'sum_small').reshape(-1)
    off = 0
    for n in SMALL:
        shp = d_small[n].shape
        sz = math.prod(shp)
        gn = tot[off:off + sz].reshape(shp)
        off += sz
        if n in SMALL_SHARDED:
            w = A[n].shape[SMALL_SHARDED[n]]
            gn = lax.dynamic_slice_in_dim(gn, chip * w, w, axis=SMALL_SHARDED[n])
        grads[n] = gn

    deltas, new_m, new_v = {}, {}, {}
    for n in W_NAMES:
        deltas[n], new_m[n], new_v[n] = _adamw(A[n], grads[n], A['m_' + n], A['v_' + n], 'adamw_' + n)
    return (loss, grad_x, *[grads[n] for n in W_NAMES], *[deltas[n] for n in W_NAMES],
            *[new_m[n] for n in W_NAMES], *[new_v[n] for n in W_NAMES])
```
